```python
import math
import jax, jax.numpy as jnp
from jax import lax
import numpy as np

D_MODEL = 1024
BATCH = 8
SEQ = 2048
DEPTH = 1

CHUNK = 128
A_GROUPS = 8
A_GROUP_DIM = D_MODEL // A_GROUPS
A_WIDTH = A_GROUPS * A_GROUP_DIM
B_HEADS = 16
B_HEAD_DIM = D_MODEL // B_HEADS
B_WIDTH = B_HEADS * B_HEAD_DIM
DILATED_PATTERNS = ((128, 1), (512, 4), (2048, 16))
N_BRANCHES = 2
D_FF = 4 * D_MODEL
EPS = 1e-6
IN_SPLITS = (A_WIDTH, 2 * A_WIDTH, 2 * A_WIDTH + B_WIDTH, 2 * A_WIDTH + 2 * B_WIDTH,
             2 * A_WIDTH + 3 * B_WIDTH, 2 * A_WIDTH + 3 * B_WIDTH + D_MODEL)
IN_COLS = 2 * A_WIDTH + 3 * B_WIDTH + N_BRANCHES * D_MODEL

kernel_name = "hybrid_gmlp_dilated_alibi_block"


def rms_norm(x, g):
    xf = x.astype(jnp.float32)
    y = xf * lax.rsqrt(jnp.mean(xf * xf, axis=-1, keepdims=True) + EPS)
    return y.astype(x.dtype) * g


def layer_norm(x, g, b):
    xf = x.astype(jnp.float32)
    mu = jnp.mean(xf, axis=-1, keepdims=True)
    var = jnp.mean(jnp.square(xf - mu), axis=-1, keepdims=True)
    return ((xf - mu) * lax.rsqrt(var + EPS)).astype(x.dtype) * g + b


def alibi_slopes(n_heads):
    return jnp.exp2(-8.0 * jnp.arange(1, n_heads + 1, dtype=jnp.float32) / n_heads)


def chunked_spatial_gating(u, v, w_s, b_s, ln_g, ln_b):
    bsz, s, _ = v.shape
    v = layer_norm(v, ln_g, ln_b)
    vc = v.reshape(bsz, s // CHUNK, CHUNK, A_GROUPS, A_GROUP_DIM)
    causal = jnp.tril(jnp.ones((CHUNK, CHUNK), dtype=bool))
    ws = jnp.where(causal[None], w_s, jnp.zeros_like(w_s))
    mixed = jnp.einsum('gts,bcsgd->bctgd', ws, vc) + b_s.T[None, None, :, :, None]
    return u * mixed.reshape(bsz, s, A_WIDTH)


def dilated_window_attention(q, k, v, slopes, window, dilation):
    bsz, s, h, dh = q.shape
    n_back = window // dilation
    blk = n_back
    span = dilation * blk
    sp = -(-s // span) * span
    pad = sp - s
    sub_len = sp // dilation
    nb = sub_len // blk

    def to_sub(t):
        t = jnp.pad(t, ((0, 0), (0, pad), (0, 0), (0, 0)))
        t = jnp.moveaxis(t.reshape(bsz, sub_len, dilation, h, dh), 2, 1)
        return t.reshape(bsz, dilation, nb, blk, h, dh)

    def with_prev(t):
        prev = jnp.pad(t, ((0, 0), (0, 0), (1, 0), (0, 0), (0, 0), (0, 0)))[:, :, :-1]
        return jnp.concatenate([prev, t], axis=3)

    qs = to_sub(q)
    kb = with_prev(to_sub(k))
    vb = with_prev(to_sub(v))
    scores = jnp.einsum('brnqhd,brnkhd->brnhqk', qs, kb).astype(jnp.float32)
    qi = jnp.arange(blk)[:, None]
    ki = jnp.arange(2 * blk)[None, :]
    delta = qi + blk - ki
    blk_idx = jnp.arange(nb)[:, None, None]
    valid = (delta >= 0) & (delta <= n_back) & (blk_idx * blk + qi - delta >= 0)
    bias = -slopes[:, None, None] * (dilation * delta).astype(jnp.float32)
    scores = jnp.where(valid[None, None, :, None], scores + bias[None, None, None], -jnp.inf)
    m = jnp.max(scores, axis=-1, keepdims=True)
    p = jnp.exp(scores - m)
    den = jnp.sum(p, axis=-1)
    out = jnp.einsum('brnhqk,brnkhd->brnqhd', p, vb.astype(jnp.float32))
    out = out / jnp.swapaxes(den, -1, -2)[..., None]
    lse = jnp.swapaxes(m[..., 0] + jnp.log(den), -1, -2)

    def from_sub(t):
        rest = t.shape[4:]
        t = jnp.moveaxis(t.reshape((bsz, dilation, sub_len) + rest), 1, 2)
        return t.reshape((bsz, sp) + rest)[:, :s]

    return from_sub(out).astype(q.dtype), from_sub(lse)


def dilated_attention_mixture(q, k, v):
    slopes = alibi_slopes(B_HEADS)
    outs, lses = [], []
    for window, dilation in DILATED_PATTERNS:
        o, l = dilated_window_attention(q, k, v, slopes, window, dilation)
        outs.append(o)
        lses.append(l)
    w = jax.nn.softmax(jnp.stack(lses, axis=-1), axis=-1)
    o = jnp.stack(outs, axis=-1).astype(jnp.float32)
    return jnp.einsum('bshdp,bshp->bshd', o, w).astype(q.dtype)


def setup_inputs(seed: int = 0) -> dict:
    key = jax.random.key(seed)
    ks = jax.random.split(key, 20)
    f32 = jnp.float32

    def nrm(k, shape, scale):
        return jax.random.normal(k, shape, f32) * scale

    def gain(k, n):
        return 1.0 + nrm(k, (DEPTH, n), 0.02)

    return {
        "x": nrm(ks[0], (BATCH, SEQ, D_MODEL), 1.0),
        "norm_mix_pre": gain(ks[1], D_MODEL),
        "w_in": nrm(ks[2], (DEPTH, D_MODEL, IN_COLS), D_MODEL ** -0.5),
        "b_gate": nrm(ks[3], (DEPTH, N_BRANCHES, D_MODEL), 0.02),
        "ln_v_g": gain(ks[4], A_WIDTH),
        "ln_v_b": nrm(ks[5], (DEPTH, A_WIDTH), 0.02),
        "w_s": nrm(ks[6], (DEPTH, A_GROUPS, CHUNK, CHUNK), 0.5 * CHUNK ** -0.5),
        "b_s": 1.0 + nrm(ks[7], (DEPTH, A_GROUPS, CHUNK), 0.02),
        "w_a_proj": nrm(ks[8], (DEPTH, A_WIDTH, D_MODEL), A_WIDTH ** -0.5),
        "w_b_proj": nrm(ks[9], (DEPTH, B_WIDTH, D_MODEL), B_WIDTH ** -0.5),
        "w_out": nrm(ks[10], (DEPTH, D_MODEL, D_MODEL), D_MODEL ** -0.5),
        "norm_mix_post": gain(ks[11], D_MODEL),
        "norm_ffn_pre": gain(ks[12], D_MODEL),
        "w_ff1": nrm(ks[13], (DEPTH, D_MODEL, D_FF), D_MODEL ** -0.5),
        "w_ff2": nrm(ks[14], (DEPTH, D_FF, D_MODEL), D_FF ** -0.5),
        "norm_ffn_post": gain(ks[15], D_MODEL),
    }


def reference(x, norm_mix_pre, w_in, b_gate, ln_v_g, ln_v_b, w_s, b_s, w_a_proj, w_b_proj,
              w_out, norm_mix_post, norm_ffn_pre, w_ff1, w_ff2, norm_ffn_post):
    bsz, s, _ = x.shape
    q_scale = 1.0 / math.sqrt(B_HEAD_DIM)
    for l in range(DEPTH):
        h = rms_norm(x, norm_mix_pre[l])
        z = h @ w_in[l]
        u_a, v_a, q, k, v_b, g_a, g_b = jnp.split(z, IN_SPLITS, axis=-1)
        y_a = chunked_spatial_gating(jax.nn.gelu(u_a), jax.nn.gelu(v_a), w_s[l], b_s[l],
                                     ln_v_g[l], ln_v_b[l])
        q = q.reshape(bsz, s, B_HEADS, B_HEAD_DIM) * q_scale
        k = k.reshape(bsz, s, B_HEADS, B_HEAD_DIM)
        v_b = v_b.reshape(bsz, s, B_HEADS, B_HEAD_DIM)
        y_b = dilated_attention_mixture(q, k, v_b).reshape(bsz, s, B_WIDTH)
        merged = (jax.nn.sigmoid(g_a + b_gate[l, 0]) * (y_a @ w_a_proj[l])
                  + jax.nn.sigmoid(g_b + b_gate[l, 1]) * (y_b @ w_b_proj[l]))
        x = x + rms_norm(merged @ w_out[l], norm_mix_post[l])
        h = rms_norm(x, norm_ffn_pre[l])
        f = jnp.square(jax.nn.relu(h @ w_ff1[l])) @ w_ff2[l]
        x = x + rms_norm(f, norm_ffn_post[l])
    return x
```

```python
import functools
import math

import jax
import jax.numpy as jnp
from jax import lax
from jax.experimental import pallas as pl
from jax.experimental.pallas import tpu as pltpu

D_MODEL = 1024
SEQ = 2048
CHUNK = 128
GROUPS = 8
GROUP_DIM = D_MODEL // GROUPS
HEADS = 16
HEAD_DIM = D_MODEL // HEADS
PATTERNS = ((128, 1), (512, 4), (2048, 16))
BLK = 128
D_FF = 4 * D_MODEL
EPS = 1e-6
N_IN_BLOCKS = 7
NEG = -1e30

VMEM_LIMIT_BYTES = 56 * 1024 * 1024

F32 = jnp.float32
BF16 = jnp.bfloat16


def _rms_norm(x, g):
    return x * lax.rsqrt(jnp.mean(x * x, axis=-1, keepdims=True) + EPS) * g


def _inproj_kernel(x_ref, g_ref, w_ref, lng_ref, lnb_ref, bg_ref, z_ref, h_ref):
    j = pl.program_id(1)

    @pl.when(j == 0)
    def _():
        h_ref[...] = _rms_norm(x_ref[...], g_ref[...]).astype(BF16)

    acc = jnp.dot(h_ref[...], w_ref[...], preferred_element_type=F32)

    @pl.when(j == 0)
    def _():
        z_ref[...] = jax.nn.gelu(acc).astype(BF16)

    @pl.when(j == 1)
    def _():
        v = jax.nn.gelu(acc)
        mu = jnp.mean(v, axis=-1, keepdims=True)
        var = jnp.mean(jnp.square(v - mu), axis=-1, keepdims=True)
        v = (v - mu) * lax.rsqrt(var + EPS) * lng_ref[...] + lnb_ref[...]
        z_ref[...] = v.astype(BF16)

    @pl.when(j == 2)
    def _():
        z_ref[...] = (acc * (1.0 / math.sqrt(HEAD_DIM))).astype(BF16)

    @pl.when((j == 3) | (j == 4))
    def _():
        z_ref[...] = acc.astype(BF16)

    @pl.when(j == 5)
    def _():
        z_ref[...] = jax.nn.sigmoid(acc + bg_ref[0:1, :]).astype(BF16)

    @pl.when(j == 6)
    def _():
        z_ref[...] = jax.nn.sigmoid(acc + bg_ref[1:2, :]).astype(BF16)


def _inproj(x2, g, w_in, ln_g, ln_b, b_gate, tm):
    t = x2.shape[0]
    return pl.pallas_call(
        _inproj_kernel,
        grid=(t // tm, N_IN_BLOCKS),
        in_specs=[
            pl.BlockSpec((tm, D_MODEL), lambda i, j: (i, 0)),
            pl.BlockSpec((1, D_MODEL), lambda i, j: (0, 0)),
            pl.BlockSpec((D_MODEL, D_MODEL), lambda i, j: (0, j)),
            pl.BlockSpec((1, D_MODEL), lambda i, j: (0, 0)),
            pl.BlockSpec((1, D_MODEL), lambda i, j: (0, 0)),
            pl.BlockSpec((2, D_MODEL), lambda i, j: (0, 0)),
        ],
        out_specs=pl.BlockSpec((tm, D_MODEL), lambda i, j: (i, j)),
        out_shape=jax.ShapeDtypeStruct((t, N_IN_BLOCKS * D_MODEL), BF16),
        scratch_shapes=[pltpu.VMEM((tm, D_MODEL), BF16)],
        compiler_params=pltpu.CompilerParams(
            dimension_semantics=("arbitrary", "arbitrary"),
            vmem_limit_bytes=VMEM_LIMIT_BYTES),
        name="inproj",
    )(x2, g, w_in, ln_g, ln_b, b_gate)


def _attn_kernel(slope_ref, q_ref, k_ref, v_ref, o_ref,
                 qf, kf, vf, bias, num0, num1, num2, den0, den1, den2, mx0, mx1, mx2):
    qf[...] = q_ref[...].astype(F32)
    kf[...] = k_ref[...].astype(F32)
    vf[...] = v_ref[...].astype(F32)

    qi = lax.broadcasted_iota(jnp.int32, (BLK, BLK), 0)
    kj = lax.broadcasted_iota(jnp.int32, (BLK, BLK), 1)
    d_cur = (qi - kj).astype(F32)
    ok_cur = kj <= qi
    ok_prev = kj >= qi
    for p, (_, dil) in enumerate(PATTERNS):
        for e in range(2):
            sl = slope_ref[e:e + 1, :]
            b_cur = jnp.where(ok_cur, -sl * (dil * d_cur), NEG)
            b_prev = jnp.where(ok_prev, -sl * (dil * (d_cur + BLK)), NEG)
            rows = slice(e * BLK, (e + 1) * BLK)
            bias[2 * p, rows, BLK:] = b_cur
            bias[2 * p, rows, :BLK] = b_prev
            bias[2 * p + 1, rows, BLK:] = b_cur
            bias[2 * p + 1, rows, :BLK] = jnp.full((BLK, BLK), NEG, F32)

    lane = lax.broadcasted_iota(jnp.int32, (BLK, 2 * HEAD_DIM), 1)
    head0 = lane < HEAD_DIM
    ones = jnp.ones((BLK, 2 * HEAD_DIM), BF16)

    def block(p, dil, start, prev_start, table, with_prev):
        rows = pl.ds(start, BLK, stride=dil) if dil > 1 else pl.ds(start, BLK)
        q = qf[rows, :]
        q2 = jnp.concatenate([jnp.where(head0, q, 0.0), jnp.where(head0, 0.0, q)],
                             axis=0).astype(BF16)
        kc = kf[rows, :].astype(BF16)
        vc = vf[rows, :].astype(BF16)
        if with_prev:
            prows = (pl.ds(prev_start, BLK, stride=dil) if dil > 1
                     else pl.ds(prev_start, BLK))
            kk = jnp.concatenate([kf[prows, :].astype(BF16), kc], axis=0)
            vv = jnp.concatenate([vf[prows, :].astype(BF16), vc], axis=0)
            b = bias[table]
        else:
            kk, vv = kc, vc
            b = bias[2 * p, :, BLK:]
        s = lax.dot_general(q2, kk, (((1,), (1,)), ((), ())),
                            preferred_element_type=F32) + b
        m = jnp.max(s, axis=-1, keepdims=True)
        pr = jnp.exp(s - m).astype(BF16)
        vx = jnp.concatenate([vv, jnp.concatenate([ones] * (vv.shape[0] // BLK), axis=0)],
                             axis=1)
        o = jnp.dot(pr, vx, preferred_element_type=F32)
        num = jnp.where(head0, o[:BLK, :BLK], o[BLK:, :BLK])
        den = jnp.where(head0, o[:BLK, BLK:], o[BLK:, BLK:])
        mm = jnp.where(head0, jnp.broadcast_to(m[:BLK], (BLK, BLK)),
                       jnp.broadcast_to(m[BLK:], (BLK, BLK)))
        return rows, num, den, mm

    def body0(n, c):
        start = pl.multiple_of(n * BLK, BLK)
        prev = pl.multiple_of(jnp.maximum(n - 1, 0) * BLK, BLK)
        table = jnp.where(n == 0, 1, 0)
        rows, num, den, mm = block(0, 1, start, prev, table, True)
        num0[rows, :] = num
        den0[rows, :] = den
        mx0[rows, :] = mm
        return c

    lax.fori_loop(0, SEQ // BLK, body0, 0)

    def body1(idx, c):
        dil = PATTERNS[1][1]
        n = idx // dil
        r = idx % dil
        start = n * (BLK * dil) + r
        prev = jnp.maximum(n - 1, 0) * (BLK * dil) + r
        table = jnp.where(n == 0, 3, 2)
        rows, num, den, mm = block(1, dil, start, prev, table, True)
        num1[rows, :] = num
        den1[rows, :] = den
        mx1[rows, :] = mm
        return c

    lax.fori_loop(0, SEQ // BLK, body1, 0)

    def body2(r, c):
        rows, num, den, mm = block(2, PATTERNS[2][1], r, r, 0, False)
        num2[rows, :] = num
        den2[rows, :] = den
        mx2[rows, :] = mm
        return c

    lax.fori_loop(0, SEQ // BLK, body2, 0)

    m_all = jnp.maximum(jnp.maximum(mx0[...], mx1[...]), mx2[...])
    w0 = jnp.exp(mx0[...] - m_all)
    w1 = jnp.exp(mx1[...] - m_all)
    w2 = jnp.exp(mx2[...] - m_all)
    num = w0 * num0[...] + w1 * num1[...] + w2 * num2[...]
    den = w0 * den0[...] + w1 * den1[...] + w2 * den2[...]
    o_ref[...] = (num / den).astype(BF16)


def _attention(z, slopes):
    t = z.shape[0]
    nb = t // SEQ
    col = 2 * HEAD_DIM
    qb, kb, vb = (2 * D_MODEL // col, 3 * D_MODEL // col, 4 * D_MODEL // col)
    full = lambda: pltpu.VMEM((SEQ, col), F32)
    return pl.pallas_call(
        _attn_kernel,
        grid=(nb, HEADS // 2),
        in_specs=[
            pl.BlockSpec((None, 2, col), lambda b, h: (h, 0, 0)),
            pl.BlockSpec((SEQ, col), lambda b, h: (b, qb + h)),
            pl.BlockSpec((SEQ, col), lambda b, h: (b, kb + h)),
            pl.BlockSpec((SEQ, col), lambda b, h: (b, vb + h)),
        ],
        out_specs=pl.BlockSpec((SEQ, col), lambda b, h: (b, h)),
        out_shape=jax.ShapeDtypeStruct((t, D_MODEL), BF16),
        scratch_shapes=[full(), full(), full(),
                        pltpu.VMEM((2 * len(PATTERNS), 2 * BLK, 2 * BLK), F32)]
                       + [full() for _ in range(9)],
        compiler_params=pltpu.CompilerParams(
            dimension_semantics=("arbitrary", "arbitrary"),
            vmem_limit_bytes=VMEM_LIMIT_BYTES),
        name="dilated_attn",
    )(slopes, z, z, z)


def _merge_kernel(u_ref, v_ref, ga_ref, gb_ref, yb_ref, x_ref, ws_ref, bs_ref,
                  wa_ref, wb_ref, wo_ref, g_ref, o_ref, ya_ref):
    tm = u_ref.shape[0]
    ti = lax.broadcasted_iota(jnp.int32, (CHUNK, CHUNK), 0)
    si = lax.broadcasted_iota(jnp.int32, (CHUNK, CHUNK), 1)
    causal = si <= ti
    for g in range(GROUPS):
        w = jnp.where(causal, ws_ref[g], 0.0).astype(BF16)
        bcol = bs_ref[:, g:g + 1]
        cols = slice(g * GROUP_DIM, (g + 1) * GROUP_DIM)
        for c in range(tm // CHUNK):
            rows = slice(c * CHUNK, (c + 1) * CHUNK)
            mixed = jnp.dot(w, v_ref[rows, cols], preferred_element_type=F32) + bcol
            ya_ref[rows, cols] = (u_ref[rows, cols].astype(F32) * mixed).astype(BF16)
    a = jnp.dot(ya_ref[...], wa_ref[...], preferred_element_type=F32)
    b = jnp.dot(yb_ref[...], wb_ref[...], preferred_element_type=F32)
    merged = ga_ref[...].astype(F32) * a + gb_ref[...].astype(F32) * b
    y = jnp.dot(merged.astype(BF16), wo_ref[...], preferred_element_type=F32)
    o_ref[...] = x_ref[...] + _rms_norm(y, g_ref[...])


def _merge(z, yb, x2, w_s, b_s_t, w_a, w_b, w_o, g, tm):
    t = x2.shape[0]
    row = lambda c: pl.BlockSpec((tm, D_MODEL), lambda i, c=c: (i, c))
    const = lambda shape: pl.BlockSpec(shape, lambda i: (0,) * len(shape))
    return pl.pallas_call(
        _merge_kernel,
        grid=(t // tm,),
        in_specs=[row(0), row(1), row(5), row(6), row(0), row(0),
                  const((GROUPS, CHUNK, CHUNK)), const((CHUNK, GROUPS)),
                  const((D_MODEL, D_MODEL)), const((D_MODEL, D_MODEL)),
                  const((D_MODEL, D_MODEL)), const((1, D_MODEL))],
        out_specs=row(0),
        out_shape=jax.ShapeDtypeStruct((t, D_MODEL), F32),
        scratch_shapes=[pltpu.VMEM((tm, D_MODEL), BF16)],
        compiler_params=pltpu.CompilerParams(
            dimension_semantics=("arbitrary",),
            vmem_limit_bytes=VMEM_LIMIT_BYTES),
        name="merge",
    )(z, z, z, z, yb, x2, w_s, b_s_t, w_a, w_b, w_o, g)


def _ffn_kernel(x_ref, gpre_ref, w1_ref, w2_ref, gpost_ref, o_ref, h_ref, acc_ref):
    j = pl.program_id(1)

    @pl.when(j == 0)
    def _():
        h_ref[...] = _rms_norm(x_ref[...], gpre_ref[...]).astype(BF16)

    f = jnp.dot(h_ref[...], w1_ref[...], preferred_element_type=F32)
    f = jnp.square(jnp.maximum(f, 0.0)).astype(BF16)
    part = jnp.dot(f, w2_ref[...], preferred_element_type=F32)

    @pl.when(j == 0)
    def _():
        acc_ref[...] = part

    @pl.when(j > 0)
    def _():
        acc_ref[...] += part

    @pl.when(j == pl.num_programs(1) - 1)
    def _():
        o_ref[...] = x_ref[...] + _rms_norm(acc_ref[...], gpost_ref[...])


def _ffn(x1, g_pre, w1, w2, g_post, tm, tf):
    t = x1.shape[0]
    return pl.pallas_call(
        _ffn_kernel,
        grid=(t // tm, D_FF // tf),
        in_specs=[
            pl.BlockSpec((tm, D_MODEL), lambda i, j: (i, 0)),
            pl.BlockSpec((1, D_MODEL), lambda i, j: (0, 0)),
            pl.BlockSpec((D_MODEL, tf), lambda i, j: (0, j)),
            pl.BlockSpec((tf, D_MODEL), lambda i, j: (j, 0)),
            pl.BlockSpec((1, D_MODEL), lambda i, j: (0, 0)),
        ],
        out_specs=pl.BlockSpec((tm, D_MODEL), lambda i, j: (i, 0)),
        out_shape=jax.ShapeDtypeStruct((t, D_MODEL), F32),
        scratch_shapes=[pltpu.VMEM((tm, D_MODEL), BF16), pltpu.VMEM((tm, D_MODEL), F32)],
        compiler_params=pltpu.CompilerParams(
            dimension_semantics=("arbitrary", "arbitrary"),
            vmem_limit_bytes=VMEM_LIMIT_BYTES),
        name="ffn",
    )(x1, g_pre, w1, w2, g_post)


def kernel(x, norm_mix_pre, w_in, b_gate, ln_v_g, ln_v_b, w_s, b_s, w_a_proj, w_b_proj,
           w_out, norm_mix_post, norm_ffn_pre, w_ff1, w_ff2, norm_ffn_post):
    bsz, s, d = x.shape
    assert (s, d) == (SEQ, D_MODEL)
    depth = w_in.shape[0]
    slopes = jnp.exp2(-8.0 * jnp.arange(1, HEADS + 1, dtype=F32) / HEADS)
    slopes = jnp.broadcast_to(slopes.reshape(HEADS // 2, 2, 1), (HEADS // 2, 2, 2 * HEAD_DIM))
    x2 = x.reshape(bsz * s, d)
    for l in range(depth):
        z = _inproj(x2, norm_mix_pre[l][None], w_in[l].astype(BF16), ln_v_g[l][None],
                    ln_v_b[l][None], b_gate[l], tm=1024)
        yb = _attention(z, slopes)
        x2 = _merge(z, yb, x2, w_s[l], b_s[l].T, w_a_proj[l].astype(BF16),
                    w_b_proj[l].astype(BF16), w_out[l].astype(BF16),
                    norm_mix_post[l][None], tm=512)
        x2 = _ffn(x2, norm_ffn_pre[l][None], w_ff1[l].astype(BF16), w_ff2[l].astype(BF16),
                  norm_ffn_post[l][None], tm=1024, tf=1024)
    return x2.reshape(bsz, s, d)
```

```python
import functools
import math

import jax
import jax.numpy as jnp
from jax import lax
from jax.experimental import pallas as pl
from jax.experimental.pallas import tpu as pltpu

D_MODEL = 1024
SEQ = 2048
CHUNK = 128
GROUPS = 8
GROUP_DIM = D_MODEL // GROUPS
HEADS = 16
HEAD_DIM = D_MODEL // HEADS
PATTERNS = ((128, 1), (512, 4), (2048, 16))
BLK = 128
D_FF = 4 * D_MODEL
EPS = 1e-6
N_IN_BLOCKS = 7
NEG = -1e30
ATTN_UNROLL = 8

VMEM_LIMIT_BYTES = 56 * 1024 * 1024

F32 = jnp.float32
BF16 = jnp.bfloat16


def _rms_norm(x, g):
    return x * lax.rsqrt(jnp.mean(x * x, axis=-1, keepdims=True) + EPS) * g


def _inproj_kernel(x_ref, g_ref, w_ref, lng_ref, lnb_ref, bg_ref, z_ref, h_ref):
    j = pl.program_id(1)

    @pl.when(j == 0)
    def _():
        h_ref[...] = _rms_norm(x_ref[...], g_ref[...]).astype(BF16)

    acc = jnp.dot(h_ref[...], w_ref[...], preferred_element_type=F32)

    @pl.when(j == 0)
    def _():
        z_ref[...] = jax.nn.gelu(acc).astype(BF16)

    @pl.when(j == 1)
    def _():
        v = jax.nn.gelu(acc)
        mu = jnp.mean(v, axis=-1, keepdims=True)
        var = jnp.mean(jnp.square(v - mu), axis=-1, keepdims=True)
        v = (v - mu) * lax.rsqrt(var + EPS) * lng_ref[...] + lnb_ref[...]
        z_ref[...] = v.astype(BF16)

    @pl.when(j == 2)
    def _():
        z_ref[...] = (acc * (1.0 / math.sqrt(HEAD_DIM))).astype(BF16)

    @pl.when((j == 3) | (j == 4))
    def _():
        z_ref[...] = acc.astype(BF16)

    @pl.when(j == 5)
    def _():
        z_ref[...] = jax.nn.sigmoid(acc + bg_ref[0:1, :]).astype(BF16)

    @pl.when(j == 6)
    def _():
        z_ref[...] = jax.nn.sigmoid(acc + bg_ref[1:2, :]).astype(BF16)


def _inproj(x2, g, w_in, ln_g, ln_b, b_gate, tm):
    t = x2.shape[0]
    return pl.pallas_call(
        _inproj_kernel,
        grid=(t // tm, N_IN_BLOCKS),
        in_specs=[
            pl.BlockSpec((tm, D_MODEL), lambda i, j: (i, 0)),
            pl.BlockSpec((1, D_MODEL), lambda i, j: (0, 0)),
            pl.BlockSpec((D_MODEL, D_MODEL), lambda i, j: (0, j)),
            pl.BlockSpec((1, D_MODEL), lambda i, j: (0, 0)),
            pl.BlockSpec((1, D_MODEL), lambda i, j: (0, 0)),
            pl.BlockSpec((2, D_MODEL), lambda i, j: (0, 0)),
        ],
        out_specs=pl.BlockSpec((tm, D_MODEL), lambda i, j: (i, j)),
        out_shape=jax.ShapeDtypeStruct((t, N_IN_BLOCKS * D_MODEL), BF16),
        scratch_shapes=[pltpu.VMEM((tm, D_MODEL), BF16)],
        compiler_params=pltpu.CompilerParams(
            dimension_semantics=("arbitrary", "arbitrary"),
            vmem_limit_bytes=VMEM_LIMIT_BYTES),
        name="inproj",
    )(x2, g, w_in, ln_g, ln_b, b_gate)


def _attn_kernel(slope_ref, q_ref, k_ref, v_ref, o_ref,
                 qf, kf, vf, bias, num0, num1, num2, den0, den1, den2, mx0, mx1, mx2):
    qf[...] = q_ref[...].astype(F32)
    kf[...] = k_ref[...].astype(F32)
    vf[...] = v_ref[...].astype(F32)

    qi = lax.broadcasted_iota(jnp.int32, (BLK, BLK), 0)
    kj = lax.broadcasted_iota(jnp.int32, (BLK, BLK), 1)
    d_cur = (qi - kj).astype(F32)
    ok_cur = kj <= qi
    ok_prev = kj >= qi
    for p, (_, dil) in enumerate(PATTERNS):
        for e in range(2):
            sl = slope_ref[e:e + 1, :]
            b_cur = jnp.where(ok_cur, -sl * (dil * d_cur), NEG)
            b_prev = jnp.where(ok_prev, -sl * (dil * (d_cur + BLK)), NEG)
            rows = slice(e * BLK, (e + 1) * BLK)
            bias[2 * p, rows, BLK:] = b_cur
            bias[2 * p, rows, :BLK] = b_prev
            bias[2 * p + 1, rows, BLK:] = b_cur
            bias[2 * p + 1, rows, :BLK] = jnp.full((BLK, BLK), NEG, F32)

    lane = lax.broadcasted_iota(jnp.int32, (BLK, 2 * HEAD_DIM), 1)
    head0 = lane < HEAD_DIM
    ones = jnp.ones((BLK, 2 * HEAD_DIM), BF16)

    def block(p, dil, start, prev_start, table, with_prev):
        rows = pl.ds(start, BLK, stride=dil) if dil > 1 else pl.ds(start, BLK)
        q = qf[rows, :]
        q2 = jnp.concatenate([jnp.where(head0, q, 0.0), jnp.where(head0, 0.0, q)],
                             axis=0).astype(BF16)
        kc = kf[rows, :].astype(BF16)
        vc = vf[rows, :].astype(BF16)
        if with_prev:
            prows = (pl.ds(prev_start, BLK, stride=dil) if dil > 1
                     else pl.ds(prev_start, BLK))
            kk = jnp.concatenate([kf[prows, :].astype(BF16), kc], axis=0)
            vv = jnp.concatenate([vf[prows, :].astype(BF16), vc], axis=0)
            b = bias[table]
        else:
            kk, vv = kc, vc
            b = bias[2 * p, :, BLK:]
        s = lax.dot_general(q2, kk, (((1,), (1,)), ((), ())),
                            preferred_element_type=F32) + b
        m = jnp.max(s, axis=-1, keepdims=True)
        pr = jnp.exp(s - m).astype(BF16)
        vx = jnp.concatenate([vv, jnp.concatenate([ones] * (vv.shape[0] // BLK), axis=0)],
                             axis=1)
        o = jnp.dot(pr, vx, preferred_element_type=F32)
        num = jnp.where(head0, o[:BLK, :BLK], o[BLK:, :BLK])
        den = jnp.where(head0, o[:BLK, BLK:], o[BLK:, BLK:])
        mm = jnp.where(head0, jnp.broadcast_to(m[:BLK], (BLK, BLK)),
                       jnp.broadcast_to(m[BLK:], (BLK, BLK)))
        return rows, num, den, mm

    def body0(n, c):
        start = pl.multiple_of(n * BLK, BLK)
        prev = pl.multiple_of(jnp.maximum(n - 1, 0) * BLK, BLK)
        table = jnp.where(n == 0, 1, 0)
        rows, num, den, mm = block(0, 1, start, prev, table, True)
        num0[rows, :] = num
        den0[rows, :] = den
        mx0[rows, :] = mm
        return c

    lax.fori_loop(0, SEQ // BLK, body0, 0, unroll=ATTN_UNROLL)

    def body1(idx, c):
        dil = PATTERNS[1][1]
        n = idx // dil
        r = idx % dil
        start = n * (BLK * dil) + r
        prev = jnp.maximum(n - 1, 0) * (BLK * dil) + r
        table = jnp.where(n == 0, 3, 2)
        rows, num, den, mm = block(1, dil, start, prev, table, True)
        num1[rows, :] = num
        den1[rows, :] = den
        mx1[rows, :] = mm
        return c

    lax.fori_loop(0, SEQ // BLK, body1, 0, unroll=ATTN_UNROLL)

    def body2(r, c):
        rows, num, den, mm = block(2, PATTERNS[2][1], r, r, 0, False)
        num2[rows, :] = num
        den2[rows, :] = den
        mx2[rows, :] = mm
        return c

    lax.fori_loop(0, SEQ // BLK, body2, 0, unroll=ATTN_UNROLL)

    m_all = jnp.maximum(jnp.maximum(mx0[...], mx1[...]), mx2[...])
    w0 = jnp.exp(mx0[...] - m_all)
    w1 = jnp.exp(mx1[...] - m_all)
    w2 = jnp.exp(mx2[...] - m_all)
    num = w0 * num0[...] + w1 * num1[...] + w2 * num2[...]
    den = w0 * den0[...] + w1 * den1[...] + w2 * den2[...]
    o_ref[...] = (num / den).astype(BF16)


def _attention(z, slopes):
    t = z.shape[0]
    nb = t // SEQ
    col = 2 * HEAD_DIM
    qb, kb, vb = (2 * D_MODEL // col, 3 * D_MODEL // col, 4 * D_MODEL // col)
    full = lambda: pltpu.VMEM((SEQ, col), F32)
    return pl.pallas_call(
        _attn_kernel,
        grid=(nb, HEADS // 2),
        in_specs=[
            pl.BlockSpec((None, 2, col), lambda b, h: (h, 0, 0)),
            pl.BlockSpec((SEQ, col), lambda b, h: (b, qb + h)),
            pl.BlockSpec((SEQ, col), lambda b, h: (b, kb + h)),
            pl.BlockSpec((SEQ, col), lambda b, h: (b, vb + h)),
        ],
        out_specs=pl.BlockSpec((SEQ, col), lambda b, h: (b, h)),
        out_shape=jax.ShapeDtypeStruct((t, D_MODEL), BF16),
        scratch_shapes=[full(), full(), full(),
                        pltpu.VMEM((2 * len(PATTERNS), 2 * BLK, 2 * BLK), F32)]
                       + [full() for _ in range(9)],
        compiler_params=pltpu.CompilerParams(
            dimension_semantics=("arbitrary", "arbitrary"),
            vmem_limit_bytes=VMEM_LIMIT_BYTES),
        name="dilated_attn",
    )(slopes, z, z, z)


def _merge_kernel(u_ref, v_ref, ga_ref, gb_ref, yb_ref, x_ref, ws_ref, bs_ref,
                  wa_ref, wb_ref, wo_ref, g_ref, o_ref, ya_ref):
    tm = u_ref.shape[0]
    ti = lax.broadcasted_iota(jnp.int32, (CHUNK, CHUNK), 0)
    si = lax.broadcasted_iota(jnp.int32, (CHUNK, CHUNK), 1)
    causal = si <= ti
    for g in range(GROUPS):
        w = jnp.where(causal, ws_ref[g], 0.0).astype(BF16)
        bcol = bs_ref[:, g:g + 1]
        cols = slice(g * GROUP_DIM, (g + 1) * GROUP_DIM)
        for c in range(tm // CHUNK):
            rows = slice(c * CHUNK, (c + 1) * CHUNK)
            mixed = jnp.dot(w, v_ref[rows, cols], preferred_element_type=F32) + bcol
            ya_ref[rows, cols] = (u_ref[rows, cols].astype(F32) * mixed).astype(BF16)
    a = jnp.dot(ya_ref[...], wa_ref[...], preferred_element_type=F32)
    b = jnp.dot(yb_ref[...], wb_ref[...], preferred_element_type=F32)
    merged = ga_ref[...].astype(F32) * a + gb_ref[...].astype(F32) * b
    y = jnp.dot(merged.astype(BF16), wo_ref[...], preferred_element_type=F32)
    o_ref[...] = x_ref[...] + _rms_norm(y, g_ref[...])


def _merge(z, yb, x2, w_s, b_s_t, w_a, w_b, w_o, g, tm):
    t = x2.shape[0]
    row = lambda c: pl.BlockSpec((tm, D_MODEL), lambda i, c=c: (i, c))
    const = lambda shape: pl.BlockSpec(shape, lambda i: (0,) * len(shape))
    return pl.pallas_call(
        _merge_kernel,
        grid=(t // tm,),
        in_specs=[row(0), row(1), row(5), row(6), row(0), row(0),
                  const((GROUPS, CHUNK, CHUNK)), const((CHUNK, GROUPS)),
                  const((D_MODEL, D_MODEL)), const((D_MODEL, D_MODEL)),
                  const((D_MODEL, D_MODEL)), const((1, D_MODEL))],
        out_specs=row(0),
        out_shape=jax.ShapeDtypeStruct((t, D_MODEL), F32),
        scratch_shapes=[pltpu.VMEM((tm, D_MODEL), BF16)],
        compiler_params=pltpu.CompilerParams(
            dimension_semantics=("arbitrary",),
            vmem_limit_bytes=VMEM_LIMIT_BYTES),
        name="merge",
    )(z, z, z, z, yb, x2, w_s, b_s_t, w_a, w_b, w_o, g)


def _ffn_kernel(x_ref, gpre_ref, w1_ref, w2_ref, gpost_ref, o_ref, h_ref, acc_ref):
    j = pl.program_id(1)

    @pl.when(j == 0)
    def _():
        h_ref[...] = _rms_norm(x_ref[...], gpre_ref[...]).astype(BF16)

    f = jnp.dot(h_ref[...], w1_ref[...], preferred_element_type=F32)
    f = jnp.square(jnp.maximum(f, 0.0)).astype(BF16)
    part = jnp.dot(f, w2_ref[...], preferred_element_type=F32)

    @pl.when(j == 0)
    def _():
        acc_ref[...] = part

    @pl.when(j > 0)
    def _():
        acc_ref[...] += part

    @pl.when(j == pl.num_programs(1) - 1)
    def _():
        o_ref[...] = x_ref[...] + _rms_norm(acc_ref[...], gpost_ref[...])


def _ffn(x1, g_pre, w1, w2, g_post, tm, tf):
    t = x1.shape[0]
    return pl.pallas_call(
        _ffn_kernel,
        grid=(t // tm, D_FF // tf),
        in_specs=[
            pl.BlockSpec((tm, D_MODEL), lambda i, j: (i, 0)),
            pl.BlockSpec((1, D_MODEL), lambda i, j: (0, 0)),
            pl.BlockSpec((D_MODEL, tf), lambda i, j: (0, j)),
            pl.BlockSpec((tf, D_MODEL), lambda i, j: (j, 0)),
            pl.BlockSpec((1, D_MODEL), lambda i, j: (0, 0)),
        ],
        out_specs=pl.BlockSpec((tm, D_MODEL), lambda i, j: (i, 0)),
        out_shape=jax.ShapeDtypeStruct((t, D_MODEL), F32),
        scratch_shapes=[pltpu.VMEM((tm, D_MODEL), BF16), pltpu.VMEM((tm, D_MODEL), F32)],
        compiler_params=pltpu.CompilerParams(
            dimension_semantics=("arbitrary", "arbitrary"),
            vmem_limit_bytes=VMEM_LIMIT_BYTES),
        name="ffn",
    )(x1, g_pre, w1, w2, g_post)


def kernel(x, norm_mix_pre, w_in, b_gate, ln_v_g, ln_v_b, w_s, b_s, w_a_proj, w_b_proj,
           w_out, norm_mix_post, norm_ffn_pre, w_ff1, w_ff2, norm_ffn_post):
    bsz, s, d = x.shape
    assert (s, d) == (SEQ, D_MODEL)
    depth = w_in.shape[0]
    slopes = jnp.exp2(-8.0 * jnp.arange(1, HEADS + 1, dtype=F32) / HEADS)
    slopes = jnp.broadcast_to(slopes.reshape(HEADS // 2, 2, 1), (HEADS // 2, 2, 2 * HEAD_DIM))
    x2 = x.reshape(bsz * s, d)
    for l in range(depth):
        z = _inproj(x2, norm_mix_pre[l][None], w_in[l].astype(BF16), ln_v_g[l][None],
                    ln_v_b[l][None], b_gate[l], tm=1024)
        yb = _attention(z, slopes)
        x2 = _merge(z, yb, x2, w_s[l], b_s[l].T, w_a_proj[l].astype(BF16),
                    w_b_proj[l].astype(BF16), w_out[l].astype(BF16),
                    norm_mix_post[l][None], tm=512)
        x2 = _ffn(x2, norm_ffn_pre[l][None], w_ff1[l].astype(BF16), w_ff2[l].astype(BF16),
                  norm_ffn_post[l][None], tm=1024, tf=1024)
    return x2.reshape(bsz, s, d)
```

```python
import functools
import math

import jax
import jax.numpy as jnp
from jax import lax
from jax.experimental import pallas as pl
from jax.experimental.pallas import tpu as pltpu

D_MODEL = 1024
SEQ = 2048
CHUNK = 128
GROUPS = 8
GROUP_DIM = D_MODEL // GROUPS
HEADS = 16
HEAD_DIM = D_MODEL // HEADS
PATTERNS = ((128, 1), (512, 4), (2048, 16))
BLK = 128
D_FF = 4 * D_MODEL
EPS = 1e-6
N_IN_BLOCKS = 7
NEG = -1e30
ATTN_UNROLL = 8

VMEM_LIMIT_BYTES = 56 * 1024 * 1024

F32 = jnp.float32
BF16 = jnp.bfloat16


def _rms_norm(x, g):
    return x * lax.rsqrt(jnp.mean(x * x, axis=-1, keepdims=True) + EPS) * g


def _sigmoid(x):
    return 0.5 * jnp.tanh(0.5 * x) + 0.5


def _inproj_kernel(x_ref, g_ref, w_ref, lng_ref, lnb_ref, bg_ref, z_ref, h_ref):
    j = pl.program_id(1)

    @pl.when(j == 0)
    def _():
        h_ref[...] = _rms_norm(x_ref[...], g_ref[...]).astype(BF16)

    def proj():
        return jnp.dot(h_ref[...], w_ref[...], preferred_element_type=F32)

    @pl.when(j == 0)
    def _():
        z_ref[...] = jax.nn.gelu(proj()).astype(BF16)

    @pl.when(j == 1)
    def _():
        v = jax.nn.gelu(proj())
        mu = jnp.mean(v, axis=-1, keepdims=True)
        var = jnp.mean(jnp.square(v - mu), axis=-1, keepdims=True)
        v = (v - mu) * lax.rsqrt(var + EPS) * lng_ref[...] + lnb_ref[...]
        z_ref[...] = v.astype(BF16)

    @pl.when(j == 2)
    def _():
        z_ref[...] = (proj() * (1.0 / math.sqrt(HEAD_DIM))).astype(BF16)

    @pl.when((j == 3) | (j == 4))
    def _():
        z_ref[...] = proj().astype(BF16)

    @pl.when(j == 5)
    def _():
        z_ref[...] = _sigmoid(proj() + bg_ref[0:1, :]).astype(BF16)

    @pl.when(j == 6)
    def _():
        z_ref[...] = _sigmoid(proj() + bg_ref[1:2, :]).astype(BF16)


def _inproj(x2, g, w_in, ln_g, ln_b, b_gate, tm):
    t = x2.shape[0]
    return pl.pallas_call(
        _inproj_kernel,
        grid=(t // tm, N_IN_BLOCKS),
        in_specs=[
            pl.BlockSpec((tm, D_MODEL), lambda i, j: (i, 0)),
            pl.BlockSpec((1, D_MODEL), lambda i, j: (0, 0)),
            pl.BlockSpec((D_MODEL, D_MODEL), lambda i, j: (0, j)),
            pl.BlockSpec((1, D_MODEL), lambda i, j: (0, 0)),
            pl.BlockSpec((1, D_MODEL), lambda i, j: (0, 0)),
            pl.BlockSpec((2, D_MODEL), lambda i, j: (0, 0)),
        ],
        out_specs=pl.BlockSpec((tm, D_MODEL), lambda i, j: (i, j)),
        out_shape=jax.ShapeDtypeStruct((t, N_IN_BLOCKS * D_MODEL), BF16),
        scratch_shapes=[pltpu.VMEM((tm, D_MODEL), BF16)],
        compiler_params=pltpu.CompilerParams(
            dimension_semantics=("arbitrary", "arbitrary"),
            vmem_limit_bytes=VMEM_LIMIT_BYTES),
        name="inproj",
    )(x2, g, w_in, ln_g, ln_b, b_gate)


def _attn_kernel(slope_ref, q_ref, k_ref, v_ref, o_ref,
                 qf, kf, vf, bias, num0, num1, num2, den0, den1, den2, mx0, mx1, mx2):
    qf[...] = q_ref[...].astype(F32)
    kf[...] = k_ref[...].astype(F32)
    vf[...] = v_ref[...].astype(F32)

    qi = lax.broadcasted_iota(jnp.int32, (BLK, BLK), 0)
    kj = lax.broadcasted_iota(jnp.int32, (BLK, BLK), 1)
    d_cur = (qi - kj).astype(F32)
    ok_cur = kj <= qi
    ok_prev = kj >= qi
    for p, (_, dil) in enumerate(PATTERNS):
        for e in range(2):
            sl = slope_ref[e:e + 1, :]
            b_cur = jnp.where(ok_cur, -sl * (dil * d_cur), NEG)
            b_prev = jnp.where(ok_prev, -sl * (dil * (d_cur + BLK)), NEG)
            rows = slice(e * BLK, (e + 1) * BLK)
            bias[2 * p, rows, BLK:] = b_cur
            bias[2 * p, rows, :BLK] = b_prev
            bias[2 * p + 1, rows, BLK:] = b_cur
            bias[2 * p + 1, rows, :BLK] = jnp.full((BLK, BLK), NEG, F32)

    lane = lax.broadcasted_iota(jnp.int32, (BLK, 2 * HEAD_DIM), 1)
    head0 = lane < HEAD_DIM
    ones = jnp.ones((BLK, 2 * HEAD_DIM), BF16)

    def block(p, dil, start, prev_start, table, with_prev):
        rows = pl.ds(start, BLK, stride=dil) if dil > 1 else pl.ds(start, BLK)
        q = qf[rows, :]
        q2 = jnp.concatenate([jnp.where(head0, q, 0.0), jnp.where(head0, 0.0, q)],
                             axis=0).astype(BF16)
        kc = kf[rows, :].astype(BF16)
        vc = vf[rows, :].astype(BF16)
        if with_prev:
            prows = (pl.ds(prev_start, BLK, stride=dil) if dil > 1
                     else pl.ds(prev_start, BLK))
            kk = jnp.concatenate([kf[prows, :].astype(BF16), kc], axis=0)
            vv = jnp.concatenate([vf[prows, :].astype(BF16), vc], axis=0)
            b = bias[table]
        else:
            kk, vv = kc, vc
            b = bias[2 * p, :, BLK:]
        s = lax.dot_general(q2, kk, (((1,), (1,)), ((), ())),
                            preferred_element_type=F32) + b
        m = jnp.max(s, axis=-1, keepdims=True)
        pr = jnp.exp(s - m).astype(BF16)
        vx = jnp.concatenate([vv, jnp.concatenate([ones] * (vv.shape[0] // BLK), axis=0)],
                             axis=1)
        o = jnp.dot(pr, vx, preferred_element_type=F32)
        num = jnp.where(head0, o[:BLK, :BLK], o[BLK:, :BLK])
        den = jnp.where(head0, o[:BLK, BLK:], o[BLK:, BLK:])
        mm = jnp.where(head0, jnp.broadcast_to(m[:BLK], (BLK, BLK)),
                       jnp.broadcast_to(m[BLK:], (BLK, BLK)))
        return rows, num, den, mm

    def body0(n, c):
        start = pl.multiple_of(n * BLK, BLK)
        prev = pl.multiple_of(jnp.maximum(n - 1, 0) * BLK, BLK)
        table = jnp.where(n == 0, 1, 0)
        rows, num, den, mm = block(0, 1, start, prev, table, True)
        num0[rows, :] = num
        den0[rows, :] = den
        mx0[rows, :] = mm
        return c

    lax.fori_loop(0, SEQ // BLK, body0, 0, unroll=ATTN_UNROLL)

    def body1(idx, c):
        dil = PATTERNS[1][1]
        n = idx // dil
        r = idx % dil
        start = n * (BLK * dil) + r
        prev = jnp.maximum(n - 1, 0) * (BLK * dil) + r
        table = jnp.where(n == 0, 3, 2)
        rows, num, den, mm = block(1, dil, start, prev, table, True)
        num1[rows, :] = num
        den1[rows, :] = den
        mx1[rows, :] = mm
        return c

    lax.fori_loop(0, SEQ // BLK, body1, 0, unroll=ATTN_UNROLL)

    def body2(r, c):
        rows, num, den, mm = block(2, PATTERNS[2][1], r, r, 0, False)
        num2[rows, :] = num
        den2[rows, :] = den
        mx2[rows, :] = mm
        return c

    lax.fori_loop(0, SEQ // BLK, body2, 0, unroll=ATTN_UNROLL)

    m_all = jnp.maximum(jnp.maximum(mx0[...], mx1[...]), mx2[...])
    w0 = jnp.exp(mx0[...] - m_all)
    w1 = jnp.exp(mx1[...] - m_all)
    w2 = jnp.exp(mx2[...] - m_all)
    num = w0 * num0[...] + w1 * num1[...] + w2 * num2[...]
    den = w0 * den0[...] + w1 * den1[...] + w2 * den2[...]
    o_ref[...] = (num / den).astype(BF16)


def _attention(z, slopes):
    t = z.shape[0]
    nb = t // SEQ
    col = 2 * HEAD_DIM
    qb, kb, vb = (2 * D_MODEL // col, 3 * D_MODEL // col, 4 * D_MODEL // col)
    full = lambda: pltpu.VMEM((SEQ, col), F32)
    return pl.pallas_call(
        _attn_kernel,
        grid=(nb, HEADS // 2),
        in_specs=[
            pl.BlockSpec((None, 2, col), lambda b, h: (h, 0, 0)),
            pl.BlockSpec((SEQ, col), lambda b, h: (b, qb + h)),
            pl.BlockSpec((SEQ, col), lambda b, h: (b, kb + h)),
            pl.BlockSpec((SEQ, col), lambda b, h: (b, vb + h)),
        ],
        out_specs=pl.BlockSpec((SEQ, col), lambda b, h: (b, h)),
        out_shape=jax.ShapeDtypeStruct((t, D_MODEL), BF16),
        scratch_shapes=[full(), full(), full(),
                        pltpu.VMEM((2 * len(PATTERNS), 2 * BLK, 2 * BLK), F32)]
                       + [full() for _ in range(9)],
        compiler_params=pltpu.CompilerParams(
            dimension_semantics=("arbitrary", "arbitrary"),
            vmem_limit_bytes=VMEM_LIMIT_BYTES),
        name="dilated_attn",
    )(slopes, z, z, z)


def _merge_kernel(u_ref, v_ref, ga_ref, gb_ref, yb_ref, x_ref, ws_ref, bs_ref,
                  wa_ref, wb_ref, wo_ref, g_ref, o_ref, ya_ref):
    tm = u_ref.shape[0]
    ti = lax.broadcasted_iota(jnp.int32, (CHUNK, CHUNK), 0)
    si = lax.broadcasted_iota(jnp.int32, (CHUNK, CHUNK), 1)
    causal = si <= ti
    for g in range(GROUPS):
        w = jnp.where(causal, ws_ref[g], 0.0).astype(BF16)
        bcol = bs_ref[:, g:g + 1]
        cols = slice(g * GROUP_DIM, (g + 1) * GROUP_DIM)
        for c in range(tm // CHUNK):
            rows = slice(c * CHUNK, (c + 1) * CHUNK)
            mixed = jnp.dot(w, v_ref[rows, cols], preferred_element_type=F32) + bcol
            ya_ref[rows, cols] = (u_ref[rows, cols].astype(F32) * mixed).astype(BF16)
    a = jnp.dot(ya_ref[...], wa_ref[...], preferred_element_type=F32)
    b = jnp.dot(yb_ref[...], wb_ref[...], preferred_element_type=F32)
    merged = ga_ref[...].astype(F32) * a + gb_ref[...].astype(F32) * b
    y = jnp.dot(merged.astype(BF16), wo_ref[...], preferred_element_type=F32)
    o_ref[...] = x_ref[...] + _rms_norm(y, g_ref[...])


def _merge(z, yb, x2, w_s, b_s_t, w_a, w_b, w_o, g, tm):
    t = x2.shape[0]
    row = lambda c: pl.BlockSpec((tm, D_MODEL), lambda i, c=c: (i, c))
    const = lambda shape: pl.BlockSpec(shape, lambda i: (0,) * len(shape))
    return pl.pallas_call(
        _merge_kernel,
        grid=(t // tm,),
        in_specs=[row(0), row(1), row(5), row(6), row(0), row(0),
                  const((GROUPS, CHUNK, CHUNK)), const((CHUNK, GROUPS)),
                  const((D_MODEL, D_MODEL)), const((D_MODEL, D_MODEL)),
                  const((D_MODEL, D_MODEL)), const((1, D_MODEL))],
        out_specs=row(0),
        out_shape=jax.ShapeDtypeStruct((t, D_MODEL), F32),
        scratch_shapes=[pltpu.VMEM((tm, D_MODEL), BF16)],
        compiler_params=pltpu.CompilerParams(
            dimension_semantics=("arbitrary",),
            vmem_limit_bytes=VMEM_LIMIT_BYTES),
        name="merge",
    )(z, z, z, z, yb, x2, w_s, b_s_t, w_a, w_b, w_o, g)


def _ffn_kernel(x_ref, gpre_ref, w1_ref, w2_ref, gpost_ref, o_ref, h_ref, acc_ref):
    j = pl.program_id(1)

    @pl.when(j == 0)
    def _():
        h_ref[...] = _rms_norm(x_ref[...], gpre_ref[...]).astype(BF16)

    f = jnp.dot(h_ref[...], w1_ref[...], preferred_element_type=F32)
    f = jnp.square(jnp.maximum(f, 0.0)).astype(BF16)
    part = jnp.dot(f, w2_ref[...], preferred_element_type=F32)

    @pl.when(j == 0)
    def _():
        acc_ref[...] = part

    @pl.when(j > 0)
    def _():
        acc_ref[...] += part

    @pl.when(j == pl.num_programs(1) - 1)
    def _():
        o_ref[...] = x_ref[...] + _rms_norm(acc_ref[...], gpost_ref[...])


def _ffn(x1, g_pre, w1, w2, g_post, tm, tf):
    t = x1.shape[0]
    return pl.pallas_call(
        _ffn_kernel,
        grid=(t // tm, D_FF // tf),
        in_specs=[
            pl.BlockSpec((tm, D_MODEL), lambda i, j: (i, 0)),
            pl.BlockSpec((1, D_MODEL), lambda i, j: (0, 0)),
            pl.BlockSpec((D_MODEL, tf), lambda i, j: (0, j)),
            pl.BlockSpec((tf, D_MODEL), lambda i, j: (j, 0)),
            pl.BlockSpec((1, D_MODEL), lambda i, j: (0, 0)),
        ],
        out_specs=pl.BlockSpec((tm, D_MODEL), lambda i, j: (i, 0)),
        out_shape=jax.ShapeDtypeStruct((t, D_MODEL), F32),
        scratch_shapes=[pltpu.VMEM((tm, D_MODEL), BF16), pltpu.VMEM((tm, D_MODEL), F32)],
        compiler_params=pltpu.CompilerParams(
            dimension_semantics=("arbitrary", "arbitrary"),
            vmem_limit_bytes=VMEM_LIMIT_BYTES),
        name="ffn",
    )(x1, g_pre, w1, w2, g_post)


def kernel(x, norm_mix_pre, w_in, b_gate, ln_v_g, ln_v_b, w_s, b_s, w_a_proj, w_b_proj,
           w_out, norm_mix_post, norm_ffn_pre, w_ff1, w_ff2, norm_ffn_post):
    bsz, s, d = x.shape
    assert (s, d) == (SEQ, D_MODEL)
    depth = w_in.shape[0]
    slopes = jnp.exp2(-8.0 * jnp.arange(1, HEADS + 1, dtype=F32) / HEADS)
    slopes = jnp.broadcast_to(slopes.reshape(HEADS // 2, 2, 1), (HEADS // 2, 2, 2 * HEAD_DIM))
    x2 = x.reshape(bsz * s, d)
    for l in range(depth):
        z = _inproj(x2, norm_mix_pre[l][None], w_in[l].astype(BF16), ln_v_g[l][None],
                    ln_v_b[l][None], b_gate[l], tm=1024)
        yb = _attention(z, slopes)
        x2 = _merge(z, yb, x2, w_s[l], b_s[l].T, w_a_proj[l].astype(BF16),
                    w_b_proj[l].astype(BF16), w_out[l].astype(BF16),
                    norm_mix_post[l][None], tm=512)
        x2 = _ffn(x2, norm_ffn_pre[l][None], w_ff1[l].astype(BF16), w_ff2[l].astype(BF16),
                  norm_ffn_post[l][None], tm=1024, tf=1024)
    return x2.reshape(bsz, s, d)
```

```python
import functools
import math

import jax
import jax.numpy as jnp
from jax import lax
from jax.experimental import pallas as pl
from jax.experimental.pallas import tpu as pltpu

D_MODEL = 1024
SEQ = 2048
CHUNK = 128
GROUPS = 8
GROUP_DIM = D_MODEL // GROUPS
HEADS = 16
HEAD_DIM = D_MODEL // HEADS
PATTERNS = ((128, 1), (512, 4), (2048, 16))
BLK = 128
D_FF = 4 * D_MODEL
EPS = 1e-6
N_IN_BLOCKS = 7
NEG = -1e30

VMEM_LIMIT_BYTES = 56 * 1024 * 1024

F32 = jnp.float32
BF16 = jnp.bfloat16


def _rms_norm(x, g):
    return x * lax.rsqrt(jnp.mean(x * x, axis=-1, keepdims=True) + EPS) * g


def _sigmoid(x):
    return 0.5 * jnp.tanh(0.5 * x) + 0.5


def _inproj_kernel(x_ref, g_ref, w_ref, lng_ref, lnb_ref, bg_ref, z_ref, h_ref):
    j = pl.program_id(1)

    @pl.when(j == 0)
    def _():
        h_ref[...] = _rms_norm(x_ref[...], g_ref[...]).astype(BF16)

    def proj():
        return jnp.dot(h_ref[...], w_ref[...], preferred_element_type=F32)

    @pl.when(j == 0)
    def _():
        z_ref[...] = jax.nn.gelu(proj()).astype(BF16)

    @pl.when(j == 1)
    def _():
        v = jax.nn.gelu(proj())
        mu = jnp.mean(v, axis=-1, keepdims=True)
        var = jnp.mean(jnp.square(v - mu), axis=-1, keepdims=True)
        v = (v - mu) * lax.rsqrt(var + EPS) * lng_ref[...] + lnb_ref[...]
        z_ref[...] = v.astype(BF16)

    @pl.when(j == 2)
    def _():
        z_ref[...] = (proj() * (1.0 / math.sqrt(HEAD_DIM))).astype(BF16)

    @pl.when((j == 3) | (j == 4))
    def _():
        z_ref[...] = proj().astype(BF16)

    @pl.when(j == 5)
    def _():
        z_ref[...] = _sigmoid(proj() + bg_ref[0:1, :]).astype(BF16)

    @pl.when(j == 6)
    def _():
        z_ref[...] = _sigmoid(proj() + bg_ref[1:2, :]).astype(BF16)


def _inproj(x2, g, w_in, ln_g, ln_b, b_gate, tm):
    t = x2.shape[0]
    return pl.pallas_call(
        _inproj_kernel,
        grid=(t // tm, N_IN_BLOCKS),
        in_specs=[
            pl.BlockSpec((tm, D_MODEL), lambda i, j: (i, 0)),
            pl.BlockSpec((1, D_MODEL), lambda i, j: (0, 0)),
            pl.BlockSpec((D_MODEL, D_MODEL), lambda i, j: (0, j)),
            pl.BlockSpec((1, D_MODEL), lambda i, j: (0, 0)),
            pl.BlockSpec((1, D_MODEL), lambda i, j: (0, 0)),
            pl.BlockSpec((2, D_MODEL), lambda i, j: (0, 0)),
        ],
        out_specs=pl.BlockSpec((tm, D_MODEL), lambda i, j: (i, j)),
        out_shape=jax.ShapeDtypeStruct((t, N_IN_BLOCKS * D_MODEL), BF16),
        scratch_shapes=[pltpu.VMEM((tm, D_MODEL), BF16)],
        compiler_params=pltpu.CompilerParams(
            dimension_semantics=("arbitrary", "arbitrary"),
            vmem_limit_bytes=VMEM_LIMIT_BYTES),
        name="inproj",
    )(x2, g, w_in, ln_g, ln_b, b_gate)


N_PAT = len(PATTERNS)
SEG4 = SEQ // 4


def _block_starts():
    general, first = [], []
    for p, (_, dil) in enumerate(PATTERNS):
        seg = SEQ // dil
        for r in range(dil):
            for n in range(seg // BLK):
                (first if n == 0 else general).append((p, p * SEQ + r * seg + n * BLK))
    return general, first


def _attn_kernel(slope_ref, q_ref, k_ref, v_ref, o_ref,
                 xf, l4, q_s, kh_s, vh_s, ee_s, bias_g, bias_f, num_s, den_s, mx_s,
                 t_num, t_den, t_mx):
    lane = lax.broadcasted_iota(jnp.int32, (BLK, 2 * HEAD_DIM), 1)
    head0 = lane < HEAD_DIM

    def emit(dst_rows, val, kind):
        if kind == 0:
            q_s[dst_rows, :] = val.astype(BF16)
        else:
            dst = kh_s if kind == 1 else vh_s
            dst[0, dst_rows, :] = jnp.where(head0, val, 0.0).astype(BF16)
            dst[1, dst_rows, :] = jnp.where(head0, 0.0, val).astype(BF16)

    for kind, src in enumerate((q_ref, k_ref, v_ref)):
        for c in range(SEQ // BLK):
            rows = pl.ds(c * BLK, BLK)
            x = src[rows, :].astype(F32)
            xf[rows, :] = x
            emit(rows, x, kind)
        for r in range(4):
            for a in range(SEG4 // BLK):
                x = xf[pl.ds(4 * BLK * a + r, BLK, stride=4), :]
                dst = r * SEG4 + a * BLK
                l4[pl.ds(dst, BLK), :] = x
                emit(pl.ds(SEQ + dst, BLK), x, kind)
        for c in range(16):
            x = l4[pl.ds((c % 4) * SEG4 + c // 4, BLK, stride=4), :]
            emit(pl.ds(2 * SEQ + c * BLK, BLK), x, kind)

    e0 = jnp.where(head0, 1.0, 0.0).astype(BF16)
    e1 = jnp.where(head0, 0.0, 1.0).astype(BF16)
    for i in range(2):
        ee_s[0, pl.ds(i * BLK, BLK), :] = e0
        ee_s[1, pl.ds(i * BLK, BLK), :] = e1

    qi = lax.broadcasted_iota(jnp.int32, (BLK, BLK), 0)
    kj = lax.broadcasted_iota(jnp.int32, (BLK, BLK), 1)
    d_cur = (qi - kj).astype(F32)
    ok_cur = kj <= qi
    ok_prev = kj >= qi
    for p, (_, dil) in enumerate(PATTERNS):
        for e in range(2):
            sl = slope_ref[e:e + 1, :]
            b_cur = jnp.where(ok_cur, -sl * (dil * d_cur), NEG)
            bias_f[p, :, e * BLK:(e + 1) * BLK] = b_cur
            if p < N_PAT - 1:
                b_prev = jnp.where(ok_prev, -sl * (dil * (d_cur + BLK)), NEG)
                bias_g[p, :, 2 * e * BLK:(2 * e + 1) * BLK] = b_prev
                bias_g[p, :, (2 * e + 1) * BLK:(2 * e + 2) * BLK] = b_cur

    def block(p, start, with_prev):
        nk = 2 * BLK if with_prev else BLK
        keys = pl.ds(start - BLK, nk) if with_prev else pl.ds(start, nk)
        q = q_s[pl.ds(start, BLK), :]
        kcat = jnp.concatenate([kh_s[0, keys, :], kh_s[1, keys, :]], axis=0)
        b = bias_g[p] if with_prev else bias_f[p]
        s = lax.dot_general(q, kcat, (((1,), (1,)), ((), ())),
                            preferred_element_type=F32) + b
        m0 = jnp.max(s[:, :nk], axis=-1, keepdims=True)
        m1 = jnp.max(s[:, nk:], axis=-1, keepdims=True)
        pr = jnp.concatenate([jnp.exp(s[:, :nk] - m0), jnp.exp(s[:, nk:] - m1)],
                             axis=1).astype(BF16)
        vcat = jnp.concatenate(
            [jnp.concatenate([vh_s[0, keys, :], vh_s[1, keys, :]], axis=0),
             jnp.concatenate([ee_s[0, pl.ds(0, nk), :], ee_s[1, pl.ds(0, nk), :]], axis=0)],
            axis=1)
        o = jnp.dot(pr, vcat, preferred_element_type=F32)
        rows = pl.ds(start, BLK)
        num_s[rows, :] = o[:, :2 * HEAD_DIM]
        den_s[rows, :] = o[:, 2 * HEAD_DIM:]
        mx_s[rows, :] = jnp.where(head0, m0, m1)

    general, first = _block_starts()
    for p, start in first:
        block(p, start, False)
    for p, start in general:
        block(p, start, True)

    for c in range(16):
        src = pl.ds(2 * SEQ + c * BLK, BLK)
        dst = pl.ds((c % 4) * SEG4 + c // 4, BLK, stride=4)
        t_num[dst, :] = num_s[src, :]
        t_den[dst, :] = den_s[src, :]
        t_mx[dst, :] = mx_s[src, :]
    for r in range(4):
        for a in range(SEG4 // BLK):
            nat = pl.ds(4 * BLK * a + r, BLK, stride=4)
            r4 = r * SEG4 + a * BLK
            x0, x1, x2 = mx_s[nat, :], mx_s[pl.ds(SEQ + r4, BLK), :], t_mx[pl.ds(r4, BLK), :]
            m_all = jnp.maximum(jnp.maximum(x0, x1), x2)
            w0, w1, w2 = jnp.exp(x0 - m_all), jnp.exp(x1 - m_all), jnp.exp(x2 - m_all)
            num = (w0 * num_s[nat, :] + w1 * num_s[pl.ds(SEQ + r4, BLK), :]
                   + w2 * t_num[pl.ds(r4, BLK), :])
            den = (w0 * den_s[nat, :] + w1 * den_s[pl.ds(SEQ + r4, BLK), :]
                   + w2 * t_den[pl.ds(r4, BLK), :])
            xf[nat, :] = num / den
    o_ref[...] = xf[...].astype(BF16)


def _attention(z, slopes):
    t = z.shape[0]
    nb = t // SEQ
    col = 2 * HEAD_DIM
    qb, kb, vb = (2 * D_MODEL // col, 3 * D_MODEL // col, 4 * D_MODEL // col)
    f32_seq = lambda: pltpu.VMEM((SEQ, col), F32)
    f32_all = lambda: pltpu.VMEM((N_PAT * SEQ, col), F32)
    return pl.pallas_call(
        _attn_kernel,
        grid=(nb, HEADS // 2),
        in_specs=[
            pl.BlockSpec((None, 2, col), lambda b, h: (h, 0, 0)),
            pl.BlockSpec((SEQ, col), lambda b, h: (b, qb + h)),
            pl.BlockSpec((SEQ, col), lambda b, h: (b, kb + h)),
            pl.BlockSpec((SEQ, col), lambda b, h: (b, vb + h)),
        ],
        out_specs=pl.BlockSpec((SEQ, col), lambda b, h: (b, h)),
        out_shape=jax.ShapeDtypeStruct((t, D_MODEL), BF16),
        scratch_shapes=[
            f32_seq(), f32_seq(),
            pltpu.VMEM((N_PAT * SEQ, col), BF16),
            pltpu.VMEM((2, N_PAT * SEQ, col), BF16),
            pltpu.VMEM((2, N_PAT * SEQ, col), BF16),
            pltpu.VMEM((2, 2 * BLK, col), BF16),
            pltpu.VMEM((N_PAT - 1, BLK, 4 * BLK), F32),
            pltpu.VMEM((N_PAT, BLK, 2 * BLK), F32),
            f32_all(), f32_all(), f32_all(),
            f32_seq(), f32_seq(), f32_seq(),
        ],
        compiler_params=pltpu.CompilerParams(
            dimension_semantics=("arbitrary", "arbitrary"),
            vmem_limit_bytes=VMEM_LIMIT_BYTES),
        name="dilated_attn",
    )(slopes, z, z, z)


def _merge_kernel(u_ref, v_ref, ga_ref, gb_ref, yb_ref, x_ref, ws_ref, bs_ref,
                  wa_ref, wb_ref, wo_ref, g_ref, o_ref, ya_ref):
    tm = u_ref.shape[0]
    ti = lax.broadcasted_iota(jnp.int32, (CHUNK, CHUNK), 0)
    si = lax.broadcasted_iota(jnp.int32, (CHUNK, CHUNK), 1)
    causal = si <= ti
    for g in range(GROUPS):
        w = jnp.where(causal, ws_ref[g], 0.0).astype(BF16)
        bcol = bs_ref[:, g:g + 1]
        cols = slice(g * GROUP_DIM, (g + 1) * GROUP_DIM)
        for c in range(tm // CHUNK):
            rows = slice(c * CHUNK, (c + 1) * CHUNK)
            mixed = jnp.dot(w, v_ref[rows, cols], preferred_element_type=F32) + bcol
            ya_ref[rows, cols] = (u_ref[rows, cols].astype(F32) * mixed).astype(BF16)
    a = jnp.dot(ya_ref[...], wa_ref[...], preferred_element_type=F32)
    b = jnp.dot(yb_ref[...], wb_ref[...], preferred_element_type=F32)
    merged = ga_ref[...].astype(F32) * a + gb_ref[...].astype(F32) * b
    y = jnp.dot(merged.astype(BF16), wo_ref[...], preferred_element_type=F32)
    o_ref[...] = x_ref[...] + _rms_norm(y, g_ref[...])


def _merge(z, yb, x2, w_s, b_s_t, w_a, w_b, w_o, g, tm):
    t = x2.shape[0]
    row = lambda c: pl.BlockSpec((tm, D_MODEL), lambda i, c=c: (i, c))
    const = lambda shape: pl.BlockSpec(shape, lambda i: (0,) * len(shape))
    return pl.pallas_call(
        _merge_kernel,
        grid=(t // tm,),
        in_specs=[row(0), row(1), row(5), row(6), row(0), row(0),
                  const((GROUPS, CHUNK, CHUNK)), const((CHUNK, GROUPS)),
                  const((D_MODEL, D_MODEL)), const((D_MODEL, D_MODEL)),
                  const((D_MODEL, D_MODEL)), const((1, D_MODEL))],
        out_specs=row(0),
        out_shape=jax.ShapeDtypeStruct((t, D_MODEL), F32),
        scratch_shapes=[pltpu.VMEM((tm, D_MODEL), BF16)],
        compiler_params=pltpu.CompilerParams(
            dimension_semantics=("arbitrary",),
            vmem_limit_bytes=VMEM_LIMIT_BYTES),
        name="merge",
    )(z, z, z, z, yb, x2, w_s, b_s_t, w_a, w_b, w_o, g)


def _ffn_kernel(x_ref, gpre_ref, w1_ref, w2_ref, gpost_ref, o_ref, h_ref, acc_ref):
    j = pl.program_id(1)

    @pl.when(j == 0)
    def _():
        h_ref[...] = _rms_norm(x_ref[...], gpre_ref[...]).astype(BF16)

    f = jnp.dot(h_ref[...], w1_ref[...], preferred_element_type=F32)
    f = jnp.square(jnp.maximum(f, 0.0)).astype(BF16)
    part = jnp.dot(f, w2_ref[...], preferred_element_type=F32)

    @pl.when(j == 0)
    def _():
        acc_ref[...] = part

    @pl.when(j > 0)
    def _():
        acc_ref[...] += part

    @pl.when(j == pl.num_programs(1) - 1)
    def _():
        o_ref[...] = x_ref[...] + _rms_norm(acc_ref[...], gpost_ref[...])


def _ffn(x1, g_pre, w1, w2, g_post, tm, tf):
    t = x1.shape[0]
    return pl.pallas_call(
        _ffn_kernel,
        grid=(t // tm, D_FF // tf),
        in_specs=[
            pl.BlockSpec((tm, D_MODEL), lambda i, j: (i, 0)),
            pl.BlockSpec((1, D_MODEL), lambda i, j: (0, 0)),
            pl.BlockSpec((D_MODEL, tf), lambda i, j: (0, j)),
            pl.BlockSpec((tf, D_MODEL), lambda i, j: (j, 0)),
            pl.BlockSpec((1, D_MODEL), lambda i, j: (0, 0)),
        ],
        out_specs=pl.BlockSpec((tm, D_MODEL), lambda i, j: (i, 0)),
        out_shape=jax.ShapeDtypeStruct((t, D_MODEL), F32),
        scratch_shapes=[pltpu.VMEM((tm, D_MODEL), BF16), pltpu.VMEM((tm, D_MODEL), F32)],
        compiler_params=pltpu.CompilerParams(
            dimension_semantics=("arbitrary", "arbitrary"),
            vmem_limit_bytes=VMEM_LIMIT_BYTES),
        name="ffn",
    )(x1, g_pre, w1, w2, g_post)


def kernel(x, norm_mix_pre, w_in, b_gate, ln_v_g, ln_v_b, w_s, b_s, w_a_proj, w_b_proj,
           w_out, norm_mix_post, norm_ffn_pre, w_ff1, w_ff2, norm_ffn_post):
    bsz, s, d = x.shape
    assert (s, d) == (SEQ, D_MODEL)
    depth = w_in.shape[0]
    slopes = jnp.exp2(-8.0 * jnp.arange(1, HEADS + 1, dtype=F32) / HEADS)
    slopes = jnp.broadcast_to(slopes.reshape(HEADS // 2, 2, 1), (HEADS // 2, 2, 2 * HEAD_DIM))
    x2 = x.reshape(bsz * s, d)
    for l in range(depth):
        z = _inproj(x2, norm_mix_pre[l][None], w_in[l].astype(BF16), ln_v_g[l][None],
                    ln_v_b[l][None], b_gate[l], tm=1024)
        yb = _attention(z, slopes)
        x2 = _merge(z, yb, x2, w_s[l], b_s[l].T, w_a_proj[l].astype(BF16),
                    w_b_proj[l].astype(BF16), w_out[l].astype(BF16),
                    norm_mix_post[l][None], tm=512)
        x2 = _ffn(x2, norm_ffn_pre[l][None], w_ff1[l].astype(BF16), w_ff2[l].astype(BF16),
                  norm_ffn_post[l][None], tm=1024, tf=1024)
    return x2.reshape(bsz, s, d)
```

```python
import functools
import math

import jax
import jax.numpy as jnp
from jax import lax
from jax.experimental import pallas as pl
from jax.experimental.pallas import tpu as pltpu

D_MODEL = 1024
SEQ = 2048
CHUNK = 128
GROUPS = 8
GROUP_DIM = D_MODEL // GROUPS
HEADS = 16
HEAD_DIM = D_MODEL // HEADS
PATTERNS = ((128, 1), (512, 4), (2048, 16))
BLK = 128
D_FF = 4 * D_MODEL
EPS = 1e-6
N_IN_BLOCKS = 7
NEG = -1e30

VMEM_LIMIT_BYTES = 56 * 1024 * 1024

F32 = jnp.float32
BF16 = jnp.bfloat16


def _rms_norm(x, g):
    return x * lax.rsqrt(jnp.mean(x * x, axis=-1, keepdims=True) + EPS) * g


def _sigmoid(x):
    return 0.5 * jnp.tanh(0.5 * x) + 0.5


def _inproj_kernel(x_ref, g_ref, w_ref, lng_ref, lnb_ref, bg_ref, z_ref, h_ref):
    j = pl.program_id(1)

    @pl.when(j == 0)
    def _():
        h_ref[...] = _rms_norm(x_ref[...], g_ref[...]).astype(BF16)

    def proj():
        return jnp.dot(h_ref[...], w_ref[...], preferred_element_type=F32)

    @pl.when(j == 0)
    def _():
        z_ref[...] = jax.nn.gelu(proj()).astype(BF16)

    @pl.when(j == 1)
    def _():
        v = jax.nn.gelu(proj())
        mu = jnp.mean(v, axis=-1, keepdims=True)
        var = jnp.mean(jnp.square(v - mu), axis=-1, keepdims=True)
        v = (v - mu) * lax.rsqrt(var + EPS) * lng_ref[...] + lnb_ref[...]
        z_ref[...] = v.astype(BF16)

    @pl.when(j == 2)
    def _():
        z_ref[...] = (proj() * (1.0 / math.sqrt(HEAD_DIM))).astype(BF16)

    @pl.when((j == 3) | (j == 4))
    def _():
        z_ref[...] = proj().astype(BF16)

    @pl.when(j == 5)
    def _():
        z_ref[...] = _sigmoid(proj() + bg_ref[0:1, :]).astype(BF16)

    @pl.when(j == 6)
    def _():
        z_ref[...] = _sigmoid(proj() + bg_ref[1:2, :]).astype(BF16)


def _inproj(x2, g, w_in, ln_g, ln_b, b_gate, tm):
    t = x2.shape[0]
    return pl.pallas_call(
        _inproj_kernel,
        grid=(t // tm, N_IN_BLOCKS),
        in_specs=[
            pl.BlockSpec((tm, D_MODEL), lambda i, j: (i, 0)),
            pl.BlockSpec((1, D_MODEL), lambda i, j: (0, 0)),
            pl.BlockSpec((D_MODEL, D_MODEL), lambda i, j: (0, j)),
            pl.BlockSpec((1, D_MODEL), lambda i, j: (0, 0)),
            pl.BlockSpec((1, D_MODEL), lambda i, j: (0, 0)),
            pl.BlockSpec((2, D_MODEL), lambda i, j: (0, 0)),
        ],
        out_specs=pl.BlockSpec((tm, D_MODEL), lambda i, j: (i, j)),
        out_shape=jax.ShapeDtypeStruct((t, N_IN_BLOCKS * D_MODEL), BF16),
        scratch_shapes=[pltpu.VMEM((tm, D_MODEL), BF16)],
        compiler_params=pltpu.CompilerParams(
            dimension_semantics=("arbitrary", "arbitrary"),
            vmem_limit_bytes=VMEM_LIMIT_BYTES),
        name="inproj",
    )(x2, g, w_in, ln_g, ln_b, b_gate)


N_PAT = len(PATTERNS)
SEG4 = SEQ // 4


def _block_starts():
    general, first = [], []
    for p, (_, dil) in enumerate(PATTERNS):
        seg = SEQ // dil
        for r in range(dil):
            for n in range(seg // BLK):
                (first if n == 0 else general).append((p, p * SEQ + r * seg + n * BLK))
    return general, first


def _attn_kernel(slope_ref, q_ref, k_ref, v_ref, o_ref,
                 xf, l4, q_s, kh_s, vh_s, ee_s, bias_g, bias_f, num_s, den_s, mx_s,
                 t_num, t_den, t_mx):
    lane = lax.broadcasted_iota(jnp.int32, (BLK, 2 * HEAD_DIM), 1)
    head0 = lane < HEAD_DIM

    top_half = lax.broadcasted_iota(jnp.int32, (2 * HEAD_DIM, BLK), 0) < HEAD_DIM

    def emit(dst_rows, val, kind):
        if kind == 0:
            q_s[dst_rows, :] = val.astype(BF16)
        elif kind == 1:
            kt = val.T
            kh_s[0, :, dst_rows] = jnp.where(top_half, kt, 0.0).astype(BF16)
            kh_s[1, :, dst_rows] = jnp.where(top_half, 0.0, kt).astype(BF16)
        else:
            vh_s[0, dst_rows, :] = jnp.where(head0, val, 0.0).astype(BF16)
            vh_s[1, dst_rows, :] = jnp.where(head0, 0.0, val).astype(BF16)

    for kind, src in enumerate((q_ref, k_ref, v_ref)):
        for c in range(SEQ // BLK):
            rows = pl.ds(c * BLK, BLK)
            x = src[rows, :].astype(F32)
            xf[rows, :] = x
            emit(rows, x, kind)
        for r in range(4):
            for a in range(SEG4 // BLK):
                x = xf[pl.ds(4 * BLK * a + r, BLK, stride=4), :]
                dst = r * SEG4 + a * BLK
                l4[pl.ds(dst, BLK), :] = x
                emit(pl.ds(SEQ + dst, BLK), x, kind)
        for c in range(16):
            x = l4[pl.ds((c % 4) * SEG4 + c // 4, BLK, stride=4), :]
            emit(pl.ds(2 * SEQ + c * BLK, BLK), x, kind)

    e0 = jnp.where(head0, 1.0, 0.0).astype(BF16)
    e1 = jnp.where(head0, 0.0, 1.0).astype(BF16)
    for i in range(2):
        ee_s[0, pl.ds(i * BLK, BLK), :] = e0
        ee_s[1, pl.ds(i * BLK, BLK), :] = e1

    qi = lax.broadcasted_iota(jnp.int32, (BLK, BLK), 0)
    kj = lax.broadcasted_iota(jnp.int32, (BLK, BLK), 1)
    d_cur = (qi - kj).astype(F32)
    ok_cur = kj <= qi
    ok_prev = kj >= qi
    for p, (_, dil) in enumerate(PATTERNS):
        for e in range(2):
            sl = slope_ref[e:e + 1, :]
            b_cur = jnp.where(ok_cur, -sl * (dil * d_cur), NEG)
            bias_f[p, :, e * BLK:(e + 1) * BLK] = b_cur
            if p < N_PAT - 1:
                b_prev = jnp.where(ok_prev, -sl * (dil * (d_cur + BLK)), NEG)
                bias_g[p, :, 2 * e * BLK:(2 * e + 1) * BLK] = b_prev
                bias_g[p, :, (2 * e + 1) * BLK:(2 * e + 2) * BLK] = b_cur

    def block(p, start, with_prev):
        nk = 2 * BLK if with_prev else BLK
        keys = pl.ds(start - BLK, nk) if with_prev else pl.ds(start, nk)
        q = q_s[pl.ds(start, BLK), :]
        kcat = jnp.concatenate([kh_s[0, :, keys], kh_s[1, :, keys]], axis=1)
        b = bias_g[p] if with_prev else bias_f[p]
        s = jnp.dot(q, kcat, preferred_element_type=F32) + b
        m0 = jnp.max(s[:, :nk], axis=-1, keepdims=True)
        m1 = jnp.max(s[:, nk:], axis=-1, keepdims=True)
        pr = jnp.concatenate([jnp.exp(s[:, :nk] - m0), jnp.exp(s[:, nk:] - m1)],
                             axis=1).astype(BF16)
        vcat = jnp.concatenate(
            [jnp.concatenate([vh_s[0, keys, :], vh_s[1, keys, :]], axis=0),
             jnp.concatenate([ee_s[0, pl.ds(0, nk), :], ee_s[1, pl.ds(0, nk), :]], axis=0)],
            axis=1)
        o = jnp.dot(pr, vcat, preferred_element_type=F32)
        rows = pl.ds(start, BLK)
        num_s[rows, :] = o[:, :2 * HEAD_DIM]
        den_s[rows, :] = o[:, 2 * HEAD_DIM:]
        mx_s[rows, :] = jnp.where(head0, m0, m1)

    general, first = _block_starts()
    for p, start in first:
        block(p, start, False)
    for p, start in general:
        block(p, start, True)

    for c in range(16):
        src = pl.ds(2 * SEQ + c * BLK, BLK)
        dst = pl.ds((c % 4) * SEG4 + c // 4, BLK, stride=4)
        t_num[dst, :] = num_s[src, :]
        t_den[dst, :] = den_s[src, :]
        t_mx[dst, :] = mx_s[src, :]
    for r in range(4):
        for a in range(SEG4 // BLK):
            nat = pl.ds(4 * BLK * a + r, BLK, stride=4)
            r4 = r * SEG4 + a * BLK
            x0, x1, x2 = mx_s[nat, :], mx_s[pl.ds(SEQ + r4, BLK), :], t_mx[pl.ds(r4, BLK), :]
            m_all = jnp.maximum(jnp.maximum(x0, x1), x2)
            w0, w1, w2 = jnp.exp(x0 - m_all), jnp.exp(x1 - m_all), jnp.exp(x2 - m_all)
            num = (w0 * num_s[nat, :] + w1 * num_s[pl.ds(SEQ + r4, BLK), :]
                   + w2 * t_num[pl.ds(r4, BLK), :])
            den = (w0 * den_s[nat, :] + w1 * den_s[pl.ds(SEQ + r4, BLK), :]
                   + w2 * t_den[pl.ds(r4, BLK), :])
            xf[nat, :] = num / den
    o_ref[...] = xf[...].astype(BF16)


def _attention(z, slopes):
    t = z.shape[0]
    nb = t // SEQ
    col = 2 * HEAD_DIM
    qb, kb, vb = (2 * D_MODEL // col, 3 * D_MODEL // col, 4 * D_MODEL // col)
    f32_seq = lambda: pltpu.VMEM((SEQ, col), F32)
    f32_all = lambda: pltpu.VMEM((N_PAT * SEQ, col), F32)
    return pl.pallas_call(
        _attn_kernel,
        grid=(nb, HEADS // 2),
        in_specs=[
            pl.BlockSpec((None, 2, col), lambda b, h: (h, 0, 0)),
            pl.BlockSpec((SEQ, col), lambda b, h: (b, qb + h)),
            pl.BlockSpec((SEQ, col), lambda b, h: (b, kb + h)),
            pl.BlockSpec((SEQ, col), lambda b, h: (b, vb + h)),
        ],
        out_specs=pl.BlockSpec((SEQ, col), lambda b, h: (b, h)),
        out_shape=jax.ShapeDtypeStruct((t, D_MODEL), BF16),
        scratch_shapes=[
            f32_seq(), f32_seq(),
            pltpu.VMEM((N_PAT * SEQ, col), BF16),
            pltpu.VMEM((2, col, N_PAT * SEQ), BF16),
            pltpu.VMEM((2, N_PAT * SEQ, col), BF16),
            pltpu.VMEM((2, 2 * BLK, col), BF16),
            pltpu.VMEM((N_PAT - 1, BLK, 4 * BLK), F32),
            pltpu.VMEM((N_PAT, BLK, 2 * BLK), F32),
            f32_all(), f32_all(), f32_all(),
            f32_seq(), f32_seq(), f32_seq(),
        ],
        compiler_params=pltpu.CompilerParams(
            dimension_semantics=("arbitrary", "arbitrary"),
            vmem_limit_bytes=VMEM_LIMIT_BYTES),
        name="dilated_attn",
    )(slopes, z, z, z)


def _merge_kernel(u_ref, v_ref, ga_ref, gb_ref, yb_ref, x_ref, ws_ref, bs_ref,
                  wa_ref, wb_ref, wo_ref, g_ref, o_ref, ya_ref):
    tm = u_ref.shape[0]
    ti = lax.broadcasted_iota(jnp.int32, (CHUNK, CHUNK), 0)
    si = lax.broadcasted_iota(jnp.int32, (CHUNK, CHUNK), 1)
    causal = si <= ti
    for g in range(GROUPS):
        w = jnp.where(causal, ws_ref[g], 0.0).astype(BF16)
        bcol = bs_ref[:, g:g + 1]
        cols = slice(g * GROUP_DIM, (g + 1) * GROUP_DIM)
        for c in range(tm // CHUNK):
            rows = slice(c * CHUNK, (c + 1) * CHUNK)
            mixed = jnp.dot(w, v_ref[rows, cols], preferred_element_type=F32) + bcol
            ya_ref[rows, cols] = (u_ref[rows, cols].astype(F32) * mixed).astype(BF16)
    a = jnp.dot(ya_ref[...], wa_ref[...], preferred_element_type=F32)
    b = jnp.dot(yb_ref[...], wb_ref[...], preferred_element_type=F32)
    merged = ga_ref[...].astype(F32) * a + gb_ref[...].astype(F32) * b
    y = jnp.dot(merged.astype(BF16), wo_ref[...], preferred_element_type=F32)
    o_ref[...] = x_ref[...] + _rms_norm(y, g_ref[...])


def _merge(z, yb, x2, w_s, b_s_t, w_a, w_b, w_o, g, tm):
    t = x2.shape[0]
    row = lambda c: pl.BlockSpec((tm, D_MODEL), lambda i, c=c: (i, c))
    const = lambda shape: pl.BlockSpec(shape, lambda i: (0,) * len(shape))
    return pl.pallas_call(
        _merge_kernel,
        grid=(t // tm,),
        in_specs=[row(0), row(1), row(5), row(6), row(0), row(0),
                  const((GROUPS, CHUNK, CHUNK)), const((CHUNK, GROUPS)),
                  const((D_MODEL, D_MODEL)), const((D_MODEL, D_MODEL)),
                  const((D_MODEL, D_MODEL)), const((1, D_MODEL))],
        out_specs=row(0),
        out_shape=jax.ShapeDtypeStruct((t, D_MODEL), F32),
        scratch_shapes=[pltpu.VMEM((tm, D_MODEL), BF16)],
        compiler_params=pltpu.CompilerParams(
            dimension_semantics=("arbitrary",),
            vmem_limit_bytes=VMEM_LIMIT_BYTES),
        name="merge",
    )(z, z, z, z, yb, x2, w_s, b_s_t, w_a, w_b, w_o, g)


def _ffn_kernel(x_ref, gpre_ref, w1_ref, w2_ref, gpost_ref, o_ref, h_ref, acc_ref):
    j = pl.program_id(1)

    @pl.when(j == 0)
    def _():
        h_ref[...] = _rms_norm(x_ref[...], gpre_ref[...]).astype(BF16)

    f = jnp.dot(h_ref[...], w1_ref[...], preferred_element_type=F32)
    f = jnp.square(jnp.maximum(f, 0.0)).astype(BF16)
    part = jnp.dot(f, w2_ref[...], preferred_element_type=F32)

    @pl.when(j == 0)
    def _():
        acc_ref[...] = part

    @pl.when(j > 0)
    def _():
        acc_ref[...] += part

    @pl.when(j == pl.num_programs(1) - 1)
    def _():
        o_ref[...] = x_ref[...] + _rms_norm(acc_ref[...], gpost_ref[...])


def _ffn(x1, g_pre, w1, w2, g_post, tm, tf):
    t = x1.shape[0]
    return pl.pallas_call(
        _ffn_kernel,
        grid=(t // tm, D_FF // tf),
        in_specs=[
            pl.BlockSpec((tm, D_MODEL), lambda i, j: (i, 0)),
            pl.BlockSpec((1, D_MODEL), lambda i, j: (0, 0)),
            pl.BlockSpec((D_MODEL, tf), lambda i, j: (0, j)),
            pl.BlockSpec((tf, D_MODEL), lambda i, j: (j, 0)),
            pl.BlockSpec((1, D_MODEL), lambda i, j: (0, 0)),
        ],
        out_specs=pl.BlockSpec((tm, D_MODEL), lambda i, j: (i, 0)),
        out_shape=jax.ShapeDtypeStruct((t, D_MODEL), F32),
        scratch_shapes=[pltpu.VMEM((tm, D_MODEL), BF16), pltpu.VMEM((tm, D_MODEL), F32)],
        compiler_params=pltpu.CompilerParams(
            dimension_semantics=("arbitrary", "arbitrary"),
            vmem_limit_bytes=VMEM_LIMIT_BYTES),
        name="ffn",
    )(x1, g_pre, w1, w2, g_post)


def kernel(x, norm_mix_pre, w_in, b_gate, ln_v_g, ln_v_b, w_s, b_s, w_a_proj, w_b_proj,
           w_out, norm_mix_post, norm_ffn_pre, w_ff1, w_ff2, norm_ffn_post):
    bsz, s, d = x.shape
    assert (s, d) == (SEQ, D_MODEL)
    depth = w_in.shape[0]
    slopes = jnp.exp2(-8.0 * jnp.arange(1, HEADS + 1, dtype=F32) / HEADS)
    slopes = jnp.broadcast_to(slopes.reshape(HEADS // 2, 2, 1), (HEADS // 2, 2, 2 * HEAD_DIM))
    x2 = x.reshape(bsz * s, d)
    for l in range(depth):
        z = _inproj(x2, norm_mix_pre[l][None], w_in[l].astype(BF16), ln_v_g[l][None],
                    ln_v_b[l][None], b_gate[l], tm=1024)
        yb = _attention(z, slopes)
        x2 = _merge(z, yb, x2, w_s[l], b_s[l].T, w_a_proj[l].astype(BF16),
                    w_b_proj[l].astype(BF16), w_out[l].astype(BF16),
                    norm_mix_post[l][None], tm=512)
        x2 = _ffn(x2, norm_ffn_pre[l][None], w_ff1[l].astype(BF16), w_ff2[l].astype(BF16),
                  norm_ffn_post[l][None], tm=1024, tf=1024)
    return x2.reshape(bsz, s, d)
```

```python
import functools
import math

import jax
import jax.numpy as jnp
from jax import lax
from jax.experimental import pallas as pl
from jax.experimental.pallas import tpu as pltpu

D_MODEL = 1024
SEQ = 2048
CHUNK = 128
GROUPS = 8
GROUP_DIM = D_MODEL // GROUPS
HEADS = 16
HEAD_DIM = D_MODEL // HEADS
PATTERNS = ((128, 1), (512, 4), (2048, 16))
BLK = 128
D_FF = 4 * D_MODEL
EPS = 1e-6
N_IN_BLOCKS = 7
NEG = -1e30

VMEM_LIMIT_BYTES = 56 * 1024 * 1024

F32 = jnp.float32
BF16 = jnp.bfloat16


def _rms_norm(x, g):
    return x * lax.rsqrt(jnp.mean(x * x, axis=-1, keepdims=True) + EPS) * g


def _sigmoid(x):
    return 0.5 * jnp.tanh(0.5 * x) + 0.5


def _inproj_kernel(x_ref, g_ref, w_ref, lng_ref, lnb_ref, bg_ref, z_ref, zq_ref, h_ref):
    j = pl.program_id(1)

    @pl.when(j == 0)
    def _():
        h_ref[...] = _rms_norm(x_ref[...], g_ref[...]).astype(BF16)

    def proj():
        return jnp.dot(h_ref[...], w_ref[...], preferred_element_type=F32)

    @pl.when(j == 0)
    def _():
        z_ref[...] = jax.nn.gelu(proj()).astype(BF16)

    @pl.when(j == 1)
    def _():
        v = jax.nn.gelu(proj())
        mu = jnp.mean(v, axis=-1, keepdims=True)
        var = jnp.mean(jnp.square(v - mu), axis=-1, keepdims=True)
        v = (v - mu) * lax.rsqrt(var + EPS) * lng_ref[...] + lnb_ref[...]
        z_ref[...] = v.astype(BF16)

    def store_heads(val):
        for hp in range(HEADS // 2):
            zq_ref[hp] = val[:, hp * 2 * HEAD_DIM:(hp + 1) * 2 * HEAD_DIM]

    @pl.when(j == 2)
    def _():
        store_heads((proj() * (1.0 / math.sqrt(HEAD_DIM))).astype(BF16))

    @pl.when((j == 3) | (j == 4))
    def _():
        store_heads(proj().astype(BF16))

    @pl.when(j == 5)
    def _():
        z_ref[...] = _sigmoid(proj() + bg_ref[0:1, :]).astype(BF16)

    @pl.when(j == 6)
    def _():
        z_ref[...] = _sigmoid(proj() + bg_ref[1:2, :]).astype(BF16)


def _inproj(x2, g, w_in, ln_g, ln_b, b_gate, tm):
    t = x2.shape[0]
    return pl.pallas_call(
        _inproj_kernel,
        grid=(t // tm, N_IN_BLOCKS),
        in_specs=[
            pl.BlockSpec((tm, D_MODEL), lambda i, j: (i, 0)),
            pl.BlockSpec((1, D_MODEL), lambda i, j: (0, 0)),
            pl.BlockSpec((D_MODEL, D_MODEL), lambda i, j: (0, j)),
            pl.BlockSpec((1, D_MODEL), lambda i, j: (0, 0)),
            pl.BlockSpec((1, D_MODEL), lambda i, j: (0, 0)),
            pl.BlockSpec((2, D_MODEL), lambda i, j: (0, 0)),
        ],
        out_specs=[
            pl.BlockSpec((tm, D_MODEL),
                         lambda i, j: (i, jnp.where(j < 2, j, jnp.maximum(j - 3, 1)))),
            pl.BlockSpec((None, HEADS // 2, tm, 2 * HEAD_DIM),
                         lambda i, j: (jnp.clip(j - 2, 0, 2), 0, i, 0)),
        ],
        out_shape=[
            jax.ShapeDtypeStruct((t, 4 * D_MODEL), BF16),
            jax.ShapeDtypeStruct((3, HEADS // 2, t, 2 * HEAD_DIM), BF16),
        ],
        scratch_shapes=[pltpu.VMEM((tm, D_MODEL), BF16)],
        compiler_params=pltpu.CompilerParams(
            dimension_semantics=("arbitrary", "arbitrary"),
            vmem_limit_bytes=VMEM_LIMIT_BYTES),
        name="inproj",
    )(x2, g, w_in, ln_g, ln_b, b_gate)


N_PAT = len(PATTERNS)
SEG4 = SEQ // 4


def _block_starts():
    general, first = [], []
    for p, (_, dil) in enumerate(PATTERNS):
        seg = SEQ // dil
        for r in range(dil):
            for n in range(seg // BLK):
                (first if n == 0 else general).append((p, p * SEQ + r * seg + n * BLK))
    return general, first


def _attn_kernel(slope_ref, q_ref, k_ref, v_ref, o_ref,
                 xf, l4, q_s, kh_s, vh_s, ee_s, bias_g, bias_f, num_s, den_s, mx_s,
                 t_num, t_den, t_mx):
    lane = lax.broadcasted_iota(jnp.int32, (BLK, 2 * HEAD_DIM), 1)
    head0 = lane < HEAD_DIM

    top_half = lax.broadcasted_iota(jnp.int32, (2 * HEAD_DIM, BLK), 0) < HEAD_DIM

    def emit(dst_rows, val, kind):
        if kind == 0:
            q_s[dst_rows, :] = val.astype(BF16)
        elif kind == 1:
            kt = val.T
            kh_s[0, :, dst_rows] = jnp.where(top_half, kt, 0.0).astype(BF16)
            kh_s[1, :, dst_rows] = jnp.where(top_half, 0.0, kt).astype(BF16)
        else:
            vh_s[0, dst_rows, :] = jnp.where(head0, val, 0.0).astype(BF16)
            vh_s[1, dst_rows, :] = jnp.where(head0, 0.0, val).astype(BF16)

    for kind, src in enumerate((q_ref, k_ref, v_ref)):
        for c in range(SEQ // BLK):
            rows = pl.ds(c * BLK, BLK)
            x = src[rows, :].astype(F32)
            xf[rows, :] = x
            emit(rows, x, kind)
        for r in range(4):
            for a in range(SEG4 // BLK):
                x = xf[pl.ds(4 * BLK * a + r, BLK, stride=4), :]
                dst = r * SEG4 + a * BLK
                l4[pl.ds(dst, BLK), :] = x
                emit(pl.ds(SEQ + dst, BLK), x, kind)
        for c in range(16):
            x = l4[pl.ds((c % 4) * SEG4 + c // 4, BLK, stride=4), :]
            emit(pl.ds(2 * SEQ + c * BLK, BLK), x, kind)

    e0 = jnp.where(head0, 1.0, 0.0).astype(BF16)
    e1 = jnp.where(head0, 0.0, 1.0).astype(BF16)
    for i in range(2):
        ee_s[0, pl.ds(i * BLK, BLK), :] = e0
        ee_s[1, pl.ds(i * BLK, BLK), :] = e1

    qi = lax.broadcasted_iota(jnp.int32, (BLK, BLK), 0)
    kj = lax.broadcasted_iota(jnp.int32, (BLK, BLK), 1)
    d_cur = (qi - kj).astype(F32)
    ok_cur = kj <= qi
    ok_prev = kj >= qi
    for p, (_, dil) in enumerate(PATTERNS):
        for e in range(2):
            sl = slope_ref[e:e + 1, :]
            b_cur = jnp.where(ok_cur, -sl * (dil * d_cur), NEG)
            bias_f[p, :, e * BLK:(e + 1) * BLK] = b_cur
            if p < N_PAT - 1:
                b_prev = jnp.where(ok_prev, -sl * (dil * (d_cur + BLK)), NEG)
                bias_g[p, :, 2 * e * BLK:(2 * e + 1) * BLK] = b_prev
                bias_g[p, :, (2 * e + 1) * BLK:(2 * e + 2) * BLK] = b_cur

    def block(p, start, with_prev):
        nk = 2 * BLK if with_prev else BLK
        keys = pl.ds(start - BLK, nk) if with_prev else pl.ds(start, nk)
        q = q_s[pl.ds(start, BLK), :]
        kcat = jnp.concatenate([kh_s[0, :, keys], kh_s[1, :, keys]], axis=1)
        b = bias_g[p] if with_prev else bias_f[p]
        s = jnp.dot(q, kcat, preferred_element_type=F32) + b
        m0 = jnp.max(s[:, :nk], axis=-1, keepdims=True)
        m1 = jnp.max(s[:, nk:], axis=-1, keepdims=True)
        pr = jnp.concatenate([jnp.exp(s[:, :nk] - m0), jnp.exp(s[:, nk:] - m1)],
                             axis=1).astype(BF16)
        vcat = jnp.concatenate(
            [jnp.concatenate([vh_s[0, keys, :], vh_s[1, keys, :]], axis=0),
             jnp.concatenate([ee_s[0, pl.ds(0, nk), :], ee_s[1, pl.ds(0, nk), :]], axis=0)],
            axis=1)
        o = jnp.dot(pr, vcat, preferred_element_type=F32)
        rows = pl.ds(start, BLK)
        num_s[rows, :] = o[:, :2 * HEAD_DIM]
        den_s[rows, :] = o[:, 2 * HEAD_DIM:]
        mx_s[rows, :] = jnp.where(head0, m0, m1)

    general, first = _block_starts()
    for p, start in first:
        block(p, start, False)
    for p, start in general:
        block(p, start, True)

    for c in range(16):
        src = pl.ds(2 * SEQ + c * BLK, BLK)
        dst = pl.ds((c % 4) * SEG4 + c // 4, BLK, stride=4)
        t_num[dst, :] = num_s[src, :]
        t_den[dst, :] = den_s[src, :]
        t_mx[dst, :] = mx_s[src, :]
    for r in range(4):
        for a in range(SEG4 // BLK):
            nat = pl.ds(4 * BLK * a + r, BLK, stride=4)
            r4 = r * SEG4 + a * BLK
            x0, x1, x2 = mx_s[nat, :], mx_s[pl.ds(SEQ + r4, BLK), :], t_mx[pl.ds(r4, BLK), :]
            m_all = jnp.maximum(jnp.maximum(x0, x1), x2)
            w0, w1, w2 = jnp.exp(x0 - m_all), jnp.exp(x1 - m_all), jnp.exp(x2 - m_all)
            num = (w0 * num_s[nat, :] + w1 * num_s[pl.ds(SEQ + r4, BLK), :]
                   + w2 * t_num[pl.ds(r4, BLK), :])
            den = (w0 * den_s[nat, :] + w1 * den_s[pl.ds(SEQ + r4, BLK), :]
                   + w2 * t_den[pl.ds(r4, BLK), :])
            xf[nat, :] = num / den
    o_ref[...] = xf[...].astype(BF16)


def _attention(zq, slopes):
    t = zq.shape[2]
    nb = t // SEQ
    col = 2 * HEAD_DIM
    qkv_spec = lambda kind: pl.BlockSpec((None, None, SEQ, col),
                                         lambda b, h, kind=kind: (kind, h, b, 0))
    f32_seq = lambda: pltpu.VMEM((SEQ, col), F32)
    f32_all = lambda: pltpu.VMEM((N_PAT * SEQ, col), F32)
    return pl.pallas_call(
        _attn_kernel,
        grid=(nb, HEADS // 2),
        in_specs=[
            pl.BlockSpec((None, 2, col), lambda b, h: (h, 0, 0)),
            qkv_spec(0), qkv_spec(1), qkv_spec(2),
        ],
        out_specs=pl.BlockSpec((None, SEQ, col), lambda b, h: (h, b, 0)),
        out_shape=jax.ShapeDtypeStruct((HEADS // 2, t, col), BF16),
        scratch_shapes=[
            f32_seq(), f32_seq(),
            pltpu.VMEM((N_PAT * SEQ, col), BF16),
            pltpu.VMEM((2, col, N_PAT * SEQ), BF16),
            pltpu.VMEM((2, N_PAT * SEQ, col), BF16),
            pltpu.VMEM((2, 2 * BLK, col), BF16),
            pltpu.VMEM((N_PAT - 1, BLK, 4 * BLK), F32),
            pltpu.VMEM((N_PAT, BLK, 2 * BLK), F32),
            f32_all(), f32_all(), f32_all(),
            f32_seq(), f32_seq(), f32_seq(),
        ],
        compiler_params=pltpu.CompilerParams(
            dimension_semantics=("arbitrary", "arbitrary"),
            vmem_limit_bytes=VMEM_LIMIT_BYTES),
        name="dilated_attn",
    )(slopes, zq, zq, zq)


def _merge_kernel(u_ref, v_ref, ga_ref, gb_ref, yb_ref, x_ref, ws_ref, bs_ref,
                  wa_ref, wb_ref, wo_ref, g_ref, o_ref, ya_ref):
    tm = u_ref.shape[0]
    ti = lax.broadcasted_iota(jnp.int32, (CHUNK, CHUNK), 0)
    si = lax.broadcasted_iota(jnp.int32, (CHUNK, CHUNK), 1)
    causal = si <= ti
    for g in range(GROUPS):
        w = jnp.where(causal, ws_ref[g], 0.0).astype(BF16)
        bcol = bs_ref[:, g:g + 1]
        cols = slice(g * GROUP_DIM, (g + 1) * GROUP_DIM)
        for c in range(tm // CHUNK):
            rows = slice(c * CHUNK, (c + 1) * CHUNK)
            mixed = jnp.dot(w, v_ref[rows, cols], preferred_element_type=F32) + bcol
            ya_ref[rows, cols] = (u_ref[rows, cols].astype(F32) * mixed).astype(BF16)
    a = jnp.dot(ya_ref[...], wa_ref[...], preferred_element_type=F32)
    yb = jnp.concatenate([yb_ref[hp] for hp in range(HEADS // 2)], axis=1)
    b = jnp.dot(yb, wb_ref[...], preferred_element_type=F32)
    merged = ga_ref[...].astype(F32) * a + gb_ref[...].astype(F32) * b
    y = jnp.dot(merged.astype(BF16), wo_ref[...], preferred_element_type=F32)
    o_ref[...] = x_ref[...] + _rms_norm(y, g_ref[...])


def _merge(z, yb, x2, w_s, b_s_t, w_a, w_b, w_o, g, tm):
    t = x2.shape[0]
    row = lambda c: pl.BlockSpec((tm, D_MODEL), lambda i, c=c: (i, c))
    const = lambda shape: pl.BlockSpec(shape, lambda i: (0,) * len(shape))
    return pl.pallas_call(
        _merge_kernel,
        grid=(t // tm,),
        in_specs=[row(0), row(1), row(2), row(3),
                  pl.BlockSpec((HEADS // 2, tm, 2 * HEAD_DIM), lambda i: (0, i, 0)), row(0),
                  const((GROUPS, CHUNK, CHUNK)), const((CHUNK, GROUPS)),
                  const((D_MODEL, D_MODEL)), const((D_MODEL, D_MODEL)),
                  const((D_MODEL, D_MODEL)), const((1, D_MODEL))],
        out_specs=row(0),
        out_shape=jax.ShapeDtypeStruct((t, D_MODEL), F32),
        scratch_shapes=[pltpu.VMEM((tm, D_MODEL), BF16)],
        compiler_params=pltpu.CompilerParams(
            dimension_semantics=("arbitrary",),
            vmem_limit_bytes=VMEM_LIMIT_BYTES),
        name="merge",
    )(z, z, z, z, yb, x2, w_s, b_s_t, w_a, w_b, w_o, g)


def _ffn_kernel(x_ref, gpre_ref, w1_ref, w2_ref, gpost_ref, o_ref, h_ref, acc_ref):
    j = pl.program_id(1)

    @pl.when(j == 0)
    def _():
        h_ref[...] = _rms_norm(x_ref[...], gpre_ref[...]).astype(BF16)

    f = jnp.dot(h_ref[...], w1_ref[...], preferred_element_type=F32)
    f = jnp.square(jnp.maximum(f, 0.0)).astype(BF16)
    part = jnp.dot(f, w2_ref[...], preferred_element_type=F32)

    @pl.when(j == 0)
    def _():
        acc_ref[...] = part

    @pl.when(j > 0)
    def _():
        acc_ref[...] += part

    @pl.when(j == pl.num_programs(1) - 1)
    def _():
        o_ref[...] = x_ref[...] + _rms_norm(acc_ref[...], gpost_ref[...])


def _ffn(x1, g_pre, w1, w2, g_post, tm, tf):
    t = x1.shape[0]
    return pl.pallas_call(
        _ffn_kernel,
        grid=(t // tm, D_FF // tf),
        in_specs=[
            pl.BlockSpec((tm, D_MODEL), lambda i, j: (i, 0)),
            pl.BlockSpec((1, D_MODEL), lambda i, j: (0, 0)),
            pl.BlockSpec((D_MODEL, tf), lambda i, j: (0, j)),
            pl.BlockSpec((tf, D_MODEL), lambda i, j: (j, 0)),
            pl.BlockSpec((1, D_MODEL), lambda i, j: (0, 0)),
        ],
        out_specs=pl.BlockSpec((tm, D_MODEL), lambda i, j: (i, 0)),
        out_shape=jax.ShapeDtypeStruct((t, D_MODEL), F32),
        scratch_shapes=[pltpu.VMEM((tm, D_MODEL), BF16), pltpu.VMEM((tm, D_MODEL), F32)],
        compiler_params=pltpu.CompilerParams(
            dimension_semantics=("arbitrary", "arbitrary"),
            vmem_limit_bytes=VMEM_LIMIT_BYTES),
        name="ffn",
    )(x1, g_pre, w1, w2, g_post)


def kernel(x, norm_mix_pre, w_in, b_gate, ln_v_g, ln_v_b, w_s, b_s, w_a_proj, w_b_proj,
           w_out, norm_mix_post, norm_ffn_pre, w_ff1, w_ff2, norm_ffn_post):
    bsz, s, d = x.shape
    assert (s, d) == (SEQ, D_MODEL)
    depth = w_in.shape[0]
    slopes = jnp.exp2(-8.0 * jnp.arange(1, HEADS + 1, dtype=F32) / HEADS)
    slopes = jnp.broadcast_to(slopes.reshape(HEADS // 2, 2, 1), (HEADS // 2, 2, 2 * HEAD_DIM))
    x2 = x.reshape(bsz * s, d)
    for l in range(depth):
        z, zq = _inproj(x2, norm_mix_pre[l][None], w_in[l].astype(BF16), ln_v_g[l][None],
                    ln_v_b[l][None], b_gate[l], tm=1024)
        yb = _attention(zq, slopes)
        x2 = _merge(z, yb, x2, w_s[l], b_s[l].T, w_a_proj[l].astype(BF16),
                    w_b_proj[l].astype(BF16), w_out[l].astype(BF16),
                    norm_mix_post[l][None], tm=512)
        x2 = _ffn(x2, norm_ffn_pre[l][None], w_ff1[l].astype(BF16), w_ff2[l].astype(BF16),
                  norm_ffn_post[l][None], tm=1024, tf=1024)
    return x2.reshape(bsz, s, d)
```

```python
import math

import jax
import jax.numpy as jnp
from jax import lax
from jax.experimental import pallas as pl
from jax.experimental.pallas import tpu as pltpu

D_MODEL = 1024
SEQ = 2048
CHUNK = 128
GROUPS = 8
GROUP_DIM = D_MODEL // GROUPS
HEADS = 16
HEAD_DIM = D_MODEL // HEADS
PATTERNS = ((128, 1), (512, 4), (2048, 16))
BLK = 128
D_FF = 4 * D_MODEL
EPS = 1e-6
N_IN_BLOCKS = 7
NEG = -1e30

VMEM_LIMIT_BYTES = 56 * 1024 * 1024

F32 = jnp.float32
BF16 = jnp.bfloat16


def _rms_norm(x, g):
    return x * lax.rsqrt(jnp.mean(x * x, axis=-1, keepdims=True) + EPS) * g


def _sigmoid(x):
    return 0.5 * jnp.tanh(0.5 * x) + 0.5


def _inproj_kernel(x_ref, g_ref, w_ref, lng_ref, lnb_ref, bg_ref, z_ref, zq_ref):
    h = _rms_norm(x_ref[...], g_ref[...]).astype(BF16)

    def proj(j):
        return jnp.dot(h, w_ref[:, j * D_MODEL:(j + 1) * D_MODEL], preferred_element_type=F32)

    def put(c, val):
        z_ref[:, c * D_MODEL:(c + 1) * D_MODEL] = val.astype(BF16)

    def put_heads(kind, val):
        val = val.astype(BF16)
        for hp in range(HEADS // 2):
            zq_ref[kind, hp] = val[:, hp * 2 * HEAD_DIM:(hp + 1) * 2 * HEAD_DIM]

    put(0, jax.nn.gelu(proj(0)))
    v = jax.nn.gelu(proj(1))
    mu = jnp.mean(v, axis=-1, keepdims=True)
    var = jnp.mean(jnp.square(v - mu), axis=-1, keepdims=True)
    put(1, (v - mu) * lax.rsqrt(var + EPS) * lng_ref[...] + lnb_ref[...])
    put_heads(0, proj(2) * (1.0 / math.sqrt(HEAD_DIM)))
    put_heads(1, proj(3))
    put_heads(2, proj(4))
    put(2, _sigmoid(proj(5) + bg_ref[0:1, :]))
    put(3, _sigmoid(proj(6) + bg_ref[1:2, :]))


def _inproj(x2, g, w_in, ln_g, ln_b, b_gate, tm):
    t = x2.shape[0]
    const = lambda shape, **kw: pl.BlockSpec(shape, lambda i: (0,) * len(shape), **kw)
    return pl.pallas_call(
        _inproj_kernel,
        grid=(t // tm,),
        in_specs=[
            pl.BlockSpec((tm, D_MODEL), lambda i: (i, 0)),
            const((1, D_MODEL)),
            const((D_MODEL, N_IN_BLOCKS * D_MODEL), pipeline_mode=pl.Buffered(1)),
            const((1, D_MODEL)),
            const((1, D_MODEL)),
            const((2, D_MODEL)),
        ],
        out_specs=[
            pl.BlockSpec((tm, 4 * D_MODEL), lambda i: (i, 0)),
            pl.BlockSpec((3, HEADS // 2, tm, 2 * HEAD_DIM), lambda i: (0, 0, i, 0)),
        ],
        out_shape=[
            jax.ShapeDtypeStruct((t, 4 * D_MODEL), BF16),
            jax.ShapeDtypeStruct((3, HEADS // 2, t, 2 * HEAD_DIM), BF16),
        ],
        compiler_params=pltpu.CompilerParams(
            dimension_semantics=("arbitrary",),
            vmem_limit_bytes=VMEM_LIMIT_BYTES),
        name="inproj",
    )(x2, g, w_in, ln_g, ln_b, b_gate)


N_PAT = len(PATTERNS)
SEG4 = SEQ // 4


def _block_starts():
    general, first = [], []
    for p, (_, dil) in enumerate(PATTERNS):
        seg = SEQ // dil
        for r in range(dil):
            for n in range(seg // BLK):
                (first if n == 0 else general).append((p, p * SEQ + r * seg + n * BLK))
    return general, first


def _attn_kernel(slope_ref, q_ref, k_ref, v_ref, o_ref,
                 xf, l4, q_s, kh_s, vh_s, ee_s, bias_g, bias_f, num_s, den_s, mx_s):
    lane = lax.broadcasted_iota(jnp.int32, (BLK, 2 * HEAD_DIM), 1)
    head0 = lane < HEAD_DIM

    top_half = lax.broadcasted_iota(jnp.int32, (2 * HEAD_DIM, BLK), 0) < HEAD_DIM

    def emit(dst_rows, val, kind):
        if kind == 0:
            q_s[dst_rows, :] = val.astype(BF16)
        elif kind == 1:
            kt = val.T
            kh_s[0, :, dst_rows] = jnp.where(top_half, kt, 0.0).astype(BF16)
            kh_s[1, :, dst_rows] = jnp.where(top_half, 0.0, kt).astype(BF16)
        else:
            vh_s[0, dst_rows, :] = jnp.where(head0, val, 0.0).astype(BF16)
            vh_s[1, dst_rows, :] = jnp.where(head0, 0.0, val).astype(BF16)

    for kind, src in enumerate((q_ref, k_ref, v_ref)):
        for c in range(SEQ // BLK):
            rows = pl.ds(c * BLK, BLK)
            x = src[rows, :].astype(F32)
            xf[rows, :] = x
            emit(rows, x, kind)
        for r in range(4):
            for a in range(SEG4 // BLK):
                x = xf[pl.ds(4 * BLK * a + r, BLK, stride=4), :]
                dst = r * SEG4 + a * BLK
                l4[pl.ds(dst, BLK), :] = x
                emit(pl.ds(SEQ + dst, BLK), x, kind)
        for c in range(16):
            x = l4[pl.ds((c % 4) * SEG4 + c // 4, BLK, stride=4), :]
            emit(pl.ds(2 * SEQ + c * BLK, BLK), x, kind)

    e0 = jnp.where(head0, 1.0, 0.0).astype(BF16)
    e1 = jnp.where(head0, 0.0, 1.0).astype(BF16)
    for i in range(2):
        ee_s[0, pl.ds(i * BLK, BLK), :] = e0
        ee_s[1, pl.ds(i * BLK, BLK), :] = e1

    qi = lax.broadcasted_iota(jnp.int32, (BLK, BLK), 0)
    kj = lax.broadcasted_iota(jnp.int32, (BLK, BLK), 1)
    d_cur = (qi - kj).astype(F32)
    ok_cur = kj <= qi
    ok_prev = kj >= qi
    for p, (_, dil) in enumerate(PATTERNS):
        for e in range(2):
            sl = slope_ref[e:e + 1, :]
            b_cur = jnp.where(ok_cur, -sl * (dil * d_cur), NEG)
            bias_f[p, :, e * BLK:(e + 1) * BLK] = b_cur
            if p < N_PAT - 1:
                b_prev = jnp.where(ok_prev, -sl * (dil * (d_cur + BLK)), NEG)
                bias_g[p, :, 2 * e * BLK:(2 * e + 1) * BLK] = b_prev
                bias_g[p, :, (2 * e + 1) * BLK:(2 * e + 2) * BLK] = b_cur

    def block(p, start, with_prev):
        nk = 2 * BLK if with_prev else BLK
        keys = pl.ds(start - BLK, nk) if with_prev else pl.ds(start, nk)
        q = q_s[pl.ds(start, BLK), :]
        kcat = jnp.concatenate([kh_s[0, :, keys], kh_s[1, :, keys]], axis=1)
        b = bias_g[p] if with_prev else bias_f[p]
        s = jnp.dot(q, kcat, preferred_element_type=F32) + b
        m0 = jnp.max(s[:, :nk], axis=-1, keepdims=True)
        m1 = jnp.max(s[:, nk:], axis=-1, keepdims=True)
        pr = jnp.concatenate([jnp.exp(s[:, :nk] - m0), jnp.exp(s[:, nk:] - m1)],
                             axis=1).astype(BF16)
        vcat = jnp.concatenate(
            [jnp.concatenate([vh_s[0, keys, :], vh_s[1, keys, :]], axis=0),
             jnp.concatenate([ee_s[0, pl.ds(0, nk), :], ee_s[1, pl.ds(0, nk), :]], axis=0)],
            axis=1)
        o = jnp.dot(pr, vcat, preferred_element_type=F32)
        if p == N_PAT - 1:
            c = (start - p * SEQ) // BLK
            rows = pl.ds(p * SEQ + (c % 4) * SEG4 + c // 4, BLK, stride=4)
        else:
            rows = pl.ds(start, BLK)
        num_s[rows, :] = o[:, :2 * HEAD_DIM]
        den_s[rows, :] = o[:, 2 * HEAD_DIM:]
        mx_s[rows, :] = jnp.where(head0, m0, m1)

    general, first = _block_starts()
    for p, start in first:
        block(p, start, False)
    for p, start in general:
        block(p, start, True)

    for r in range(4):
        for a in range(SEG4 // BLK):
            nat = pl.ds(4 * BLK * a + r, BLK, stride=4)
            r4 = r * SEG4 + a * BLK
            p1, p2 = pl.ds(SEQ + r4, BLK), pl.ds(2 * SEQ + r4, BLK)
            x0, x1, x2 = mx_s[nat, :], mx_s[p1, :], mx_s[p2, :]
            m_all = jnp.maximum(jnp.maximum(x0, x1), x2)
            w0, w1, w2 = jnp.exp(x0 - m_all), jnp.exp(x1 - m_all), jnp.exp(x2 - m_all)
            num = w0 * num_s[nat, :] + w1 * num_s[p1, :] + w2 * num_s[p2, :]
            den = w0 * den_s[nat, :] + w1 * den_s[p1, :] + w2 * den_s[p2, :]
            xf[nat, :] = num / den
    o_ref[...] = xf[...].astype(BF16)


def _attention(zq, slopes):
    t = zq.shape[2]
    nb = t // SEQ
    col = 2 * HEAD_DIM
    qkv_spec = lambda kind: pl.BlockSpec((None, None, SEQ, col),
                                         lambda b, h, kind=kind: (kind, h, b, 0))
    f32_seq = lambda: pltpu.VMEM((SEQ, col), F32)
    f32_all = lambda: pltpu.VMEM((N_PAT * SEQ, col), F32)
    return pl.pallas_call(
        _attn_kernel,
        grid=(nb, HEADS // 2),
        in_specs=[
            pl.BlockSpec((None, 2, col), lambda b, h: (h, 0, 0)),
            qkv_spec(0), qkv_spec(1), qkv_spec(2),
        ],
        out_specs=pl.BlockSpec((None, SEQ, col), lambda b, h: (h, b, 0)),
        out_shape=jax.ShapeDtypeStruct((HEADS // 2, t, col), BF16),
        scratch_shapes=[
            f32_seq(), f32_seq(),
            pltpu.VMEM((N_PAT * SEQ, col), BF16),
            pltpu.VMEM((2, col, N_PAT * SEQ), BF16),
            pltpu.VMEM((2, N_PAT * SEQ, col), BF16),
            pltpu.VMEM((2, 2 * BLK, col), BF16),
            pltpu.VMEM((N_PAT - 1, BLK, 4 * BLK), F32),
            pltpu.VMEM((N_PAT, BLK, 2 * BLK), F32),
            f32_all(), f32_all(), f32_all(),
        ],
        compiler_params=pltpu.CompilerParams(
            dimension_semantics=("arbitrary", "arbitrary"),
            vmem_limit_bytes=VMEM_LIMIT_BYTES),
        name="dilated_attn",
    )(slopes, zq, zq, zq)


def _merge_kernel(u_ref, v_ref, ga_ref, gb_ref, yb_ref, x_ref, ws_ref, bs_ref,
                  wa_ref, wb_ref, wo_ref, g_ref, o_ref, ya_ref):
    tm = u_ref.shape[0]
    ti = lax.broadcasted_iota(jnp.int32, (CHUNK, CHUNK), 0)
    si = lax.broadcasted_iota(jnp.int32, (CHUNK, CHUNK), 1)
    causal = si <= ti
    for g in range(GROUPS):
        w = jnp.where(causal, ws_ref[g], 0.0).astype(BF16)
        bcol = bs_ref[:, g:g + 1]
        cols = slice(g * GROUP_DIM, (g + 1) * GROUP_DIM)
        for c in range(tm // CHUNK):
            rows = slice(c * CHUNK, (c + 1) * CHUNK)
            mixed = jnp.dot(w, v_ref[rows, cols], preferred_element_type=F32) + bcol
            ya_ref[rows, cols] = (u_ref[rows, cols].astype(F32) * mixed).astype(BF16)
    a = jnp.dot(ya_ref[...], wa_ref[...], preferred_element_type=F32)
    yb = jnp.concatenate([yb_ref[hp] for hp in range(HEADS // 2)], axis=1)
    b = jnp.dot(yb, wb_ref[...], preferred_element_type=F32)
    merged = ga_ref[...].astype(F32) * a + gb_ref[...].astype(F32) * b
    y = jnp.dot(merged.astype(BF16), wo_ref[...], preferred_element_type=F32)
    o_ref[...] = x_ref[...] + _rms_norm(y, g_ref[...])


def _merge(z, yb, x2, w_s, b_s_t, w_a, w_b, w_o, g, tm):
    t = x2.shape[0]
    row = lambda c: pl.BlockSpec((tm, D_MODEL), lambda i, c=c: (i, c))
    const = lambda shape: pl.BlockSpec(shape, lambda i: (0,) * len(shape))
    return pl.pallas_call(
        _merge_kernel,
        grid=(t // tm,),
        in_specs=[row(0), row(1), row(2), row(3),
                  pl.BlockSpec((HEADS // 2, tm, 2 * HEAD_DIM), lambda i: (0, i, 0)), row(0),
                  const((GROUPS, CHUNK, CHUNK)), const((CHUNK, GROUPS)),
                  const((D_MODEL, D_MODEL)), const((D_MODEL, D_MODEL)),
                  const((D_MODEL, D_MODEL)), const((1, D_MODEL))],
        out_specs=row(0),
        out_shape=jax.ShapeDtypeStruct((t, D_MODEL), F32),
        scratch_shapes=[pltpu.VMEM((tm, D_MODEL), BF16)],
        compiler_params=pltpu.CompilerParams(
            dimension_semantics=("arbitrary",),
            vmem_limit_bytes=VMEM_LIMIT_BYTES),
        name="merge",
    )(z, z, z, z, yb, x2, w_s, b_s_t, w_a, w_b, w_o, g)


def _ffn_kernel(x_ref, gpre_ref, w1_ref, w2_ref, gpost_ref, o_ref):
    x = x_ref[...]
    h = _rms_norm(x, gpre_ref[...]).astype(BF16)
    f = jnp.dot(h, w1_ref[...], preferred_element_type=F32)
    f = jnp.square(jnp.maximum(f, 0.0)).astype(BF16)
    y = jnp.dot(f, w2_ref[...], preferred_element_type=F32)
    o_ref[...] = x + _rms_norm(y, gpost_ref[...])


def _ffn(x1, g_pre, w1, w2, g_post, tm):
    t = x1.shape[0]
    const = lambda shape, **kw: pl.BlockSpec(shape, lambda i: (0,) * len(shape), **kw)
    return pl.pallas_call(
        _ffn_kernel,
        grid=(t // tm,),
        in_specs=[
            pl.BlockSpec((tm, D_MODEL), lambda i: (i, 0)),
            const((1, D_MODEL)),
            const((D_MODEL, D_FF), pipeline_mode=pl.Buffered(1)),
            const((D_FF, D_MODEL), pipeline_mode=pl.Buffered(1)),
            const((1, D_MODEL)),
        ],
        out_specs=pl.BlockSpec((tm, D_MODEL), lambda i: (i, 0)),
        out_shape=jax.ShapeDtypeStruct((t, D_MODEL), F32),
        compiler_params=pltpu.CompilerParams(
            dimension_semantics=("arbitrary",),
            vmem_limit_bytes=VMEM_LIMIT_BYTES),
        name="ffn",
    )(x1, g_pre, w1, w2, g_post)


def kernel(x, norm_mix_pre, w_in, b_gate, ln_v_g, ln_v_b, w_s, b_s, w_a_proj, w_b_proj,
           w_out, norm_mix_post, norm_ffn_pre, w_ff1, w_ff2, norm_ffn_post):
    bsz, s, d = x.shape
    assert (s, d) == (SEQ, D_MODEL)
    depth = w_in.shape[0]
    slopes = jnp.exp2(-8.0 * jnp.arange(1, HEADS + 1, dtype=F32) / HEADS)
    slopes = jnp.broadcast_to(slopes.reshape(HEADS // 2, 2, 1), (HEADS // 2, 2, 2 * HEAD_DIM))
    x2 = x.reshape(bsz * s, d)
    for l in range(depth):
        z, zq = _inproj(x2, norm_mix_pre[l][None], w_in[l].astype(BF16), ln_v_g[l][None],
                    ln_v_b[l][None], b_gate[l], tm=512)
        yb = _attention(zq, slopes)
        x2 = _merge(z, yb, x2, w_s[l], b_s[l].T, w_a_proj[l].astype(BF16),
                    w_b_proj[l].astype(BF16), w_out[l].astype(BF16),
                    norm_mix_post[l][None], tm=512)
        x2 = _ffn(x2, norm_ffn_pre[l][None], w_ff1[l].astype(BF16), w_ff2[l].astype(BF16),
                  norm_ffn_post[l][None], tm=512)
    return x2.reshape(bsz, s, d)
```

```python
import math

import jax
import jax.numpy as jnp
from jax import lax
from jax.experimental import pallas as pl
from jax.experimental.pallas import tpu as pltpu

D_MODEL = 1024
SEQ = 2048
CHUNK = 128
GROUPS = 8
GROUP_DIM = D_MODEL // GROUPS
HEADS = 16
HEAD_DIM = D_MODEL // HEADS
PATTERNS = ((128, 1), (512, 4), (2048, 16))
BLK = 128
D_FF = 4 * D_MODEL
EPS = 1e-6
NEG = -1e30

VMEM_LIMIT_BYTES = 56 * 1024 * 1024

F32 = jnp.float32
BF16 = jnp.bfloat16


def _rms_norm(x, g):
    return x * lax.rsqrt(jnp.mean(x * x, axis=-1, keepdims=True) + EPS) * g


def _sigmoid(x):
    return 0.5 * jnp.tanh(0.5 * x) + 0.5


def _inproj_kernel(x_ref, g_ref, wuv_ref, wga_ref, wgb_ref, lng_ref, lnb_ref, bg_ref,
                   z_ref, h_ref):
    h = _rms_norm(x_ref[...], g_ref[...]).astype(BF16)
    h_ref[...] = h

    def put(c, val):
        z_ref[:, c * D_MODEL:(c + 1) * D_MODEL] = val.astype(BF16)

    uv = lambda j: jnp.dot(h, wuv_ref[:, j * D_MODEL:(j + 1) * D_MODEL],
                           preferred_element_type=F32)
    put(0, jax.nn.gelu(uv(0)))
    v = jax.nn.gelu(uv(1))
    mu = jnp.mean(v, axis=-1, keepdims=True)
    var = jnp.mean(jnp.square(v - mu), axis=-1, keepdims=True)
    put(1, (v - mu) * lax.rsqrt(var + EPS) * lng_ref[...] + lnb_ref[...])
    ga = jnp.dot(h, wga_ref[...], preferred_element_type=F32)
    put(2, _sigmoid(ga + bg_ref[0:1, :]))
    gb = jnp.dot(h, wgb_ref[...], preferred_element_type=F32)
    put(3, _sigmoid(gb + bg_ref[1:2, :]))


def _inproj(x2, g, w_in, ln_g, ln_b, b_gate, tm):
    t = x2.shape[0]
    const = lambda shape, **kw: pl.BlockSpec(shape, lambda i: (0,) * len(shape), **kw)
    wcols = lambda width, blk: pl.BlockSpec((D_MODEL, width), lambda i, blk=blk: (0, blk),
                                            pipeline_mode=pl.Buffered(1))
    return pl.pallas_call(
        _inproj_kernel,
        grid=(t // tm,),
        in_specs=[
            pl.BlockSpec((tm, D_MODEL), lambda i: (i, 0)),
            const((1, D_MODEL)),
            wcols(2 * D_MODEL, 0),
            wcols(D_MODEL, 5),
            wcols(D_MODEL, 6),
            const((1, D_MODEL)),
            const((1, D_MODEL)),
            const((2, D_MODEL)),
        ],
        out_specs=[
            pl.BlockSpec((tm, 4 * D_MODEL), lambda i: (i, 0)),
            pl.BlockSpec((tm, D_MODEL), lambda i: (i, 0)),
        ],
        out_shape=[
            jax.ShapeDtypeStruct((t, 4 * D_MODEL), BF16),
            jax.ShapeDtypeStruct((t, D_MODEL), BF16),
        ],
        compiler_params=pltpu.CompilerParams(
            dimension_semantics=("arbitrary",),
            vmem_limit_bytes=VMEM_LIMIT_BYTES),
        name="inproj",
    )(x2, g, w_in, w_in, w_in, ln_g, ln_b, b_gate)


N_PAT = len(PATTERNS)
SEG4 = SEQ // 4
PROJ_ROWS = 256


def _block_starts():
    general, first = [], []
    for p, (_, dil) in enumerate(PATTERNS):
        seg = SEQ // dil
        for r in range(dil):
            for n in range(seg // BLK):
                (first if n == 0 else general).append((p, p * SEQ + r * seg + n * BLK))
    return general, first


def _attn_kernel(slope_ref, h_ref, wq_ref, wk_ref, wv_ref, o_ref,
                 xq, xk, xv, l4, q_s, kh_s, vh_s, ee_s, bias_g, bias_f, num_s, den_s, mx_s):
    lane = lax.broadcasted_iota(jnp.int32, (BLK, 2 * HEAD_DIM), 1)
    head0 = lane < HEAD_DIM
    col = 2 * HEAD_DIM

    @pl.when(pl.program_id(0) == 0)
    def _():
        for xf in (xq, xk, xv):
            xf[...] = jnp.zeros((SEQ, col), F32)

    top_half = lax.broadcasted_iota(jnp.int32, (2 * HEAD_DIM, BLK), 0) < HEAD_DIM

    def emit(dst_rows, val, kind):
        if kind == 0:
            q_s[dst_rows, :] = val.astype(BF16)
        elif kind == 1:
            kt = val.T
            kh_s[0, :, dst_rows] = jnp.where(top_half, kt, 0.0).astype(BF16)
            kh_s[1, :, dst_rows] = jnp.where(top_half, 0.0, kt).astype(BF16)
        else:
            vh_s[0, dst_rows, :] = jnp.where(head0, val, 0.0).astype(BF16)
            vh_s[1, dst_rows, :] = jnp.where(head0, 0.0, val).astype(BF16)

    for kind, xf in enumerate((xq, xk, xv)):
        for c in range(SEQ // BLK):
            rows = pl.ds(c * BLK, BLK)
            emit(rows, xf[rows, :], kind)
        for r in range(4):
            for a in range(SEG4 // BLK):
                x = xf[pl.ds(4 * BLK * a + r, BLK, stride=4), :]
                dst = r * SEG4 + a * BLK
                l4[pl.ds(dst, BLK), :] = x
                emit(pl.ds(SEQ + dst, BLK), x, kind)
        for c in range(16):
            x = l4[pl.ds((c % 4) * SEG4 + c // 4, BLK, stride=4), :]
            emit(pl.ds(2 * SEQ + c * BLK, BLK), x, kind)

    e0 = jnp.where(head0, 1.0, 0.0).astype(BF16)
    e1 = jnp.where(head0, 0.0, 1.0).astype(BF16)
    for i in range(2):
        ee_s[0, pl.ds(i * BLK, BLK), :] = e0
        ee_s[1, pl.ds(i * BLK, BLK), :] = e1

    qi = lax.broadcasted_iota(jnp.int32, (BLK, BLK), 0)
    kj = lax.broadcasted_iota(jnp.int32, (BLK, BLK), 1)
    d_cur = (qi - kj).astype(F32)
    ok_cur = kj <= qi
    ok_prev = kj >= qi
    for p, (_, dil) in enumerate(PATTERNS):
        for e in range(2):
            sl = slope_ref[e:e + 1, :]
            b_cur = jnp.where(ok_cur, -sl * (dil * d_cur), NEG)
            bias_f[p, :, e * BLK:(e + 1) * BLK] = b_cur
            if p < N_PAT - 1:
                b_prev = jnp.where(ok_prev, -sl * (dil * (d_cur + BLK)), NEG)
                bias_g[p, :, 2 * e * BLK:(2 * e + 1) * BLK] = b_prev
                bias_g[p, :, (2 * e + 1) * BLK:(2 * e + 2) * BLK] = b_cur

    def scores(p, start, with_prev):
        nk = 2 * BLK if with_prev else BLK
        keys = pl.ds(start - BLK, nk) if with_prev else pl.ds(start, nk)
        q = q_s[pl.ds(start, BLK), :]
        kcat = jnp.concatenate([kh_s[0, :, keys], kh_s[1, :, keys]], axis=1)
        b = bias_g[p] if with_prev else bias_f[p]
        return jnp.dot(q, kcat, preferred_element_type=F32) + b

    def attend(s, p, start, with_prev):
        nk = 2 * BLK if with_prev else BLK
        keys = pl.ds(start - BLK, nk) if with_prev else pl.ds(start, nk)
        m0 = jnp.max(s[:, :nk], axis=-1, keepdims=True)
        m1 = jnp.max(s[:, nk:], axis=-1, keepdims=True)
        pr = jnp.concatenate([jnp.exp(s[:, :nk] - m0), jnp.exp(s[:, nk:] - m1)],
                             axis=1).astype(BF16)
        vcat = jnp.concatenate(
            [jnp.concatenate([vh_s[0, keys, :], vh_s[1, keys, :]], axis=0),
             jnp.concatenate([ee_s[0, pl.ds(0, nk), :], ee_s[1, pl.ds(0, nk), :]], axis=0)],
            axis=1)
        o = jnp.dot(pr, vcat, preferred_element_type=F32)
        if p == N_PAT - 1:
            c = (start - p * SEQ) // BLK
            rows = pl.ds(p * SEQ + (c % 4) * SEG4 + c // 4, BLK, stride=4)
        else:
            rows = pl.ds(start, BLK)
        num_s[rows, :] = o[:, :2 * HEAD_DIM]
        den_s[rows, :] = o[:, 2 * HEAD_DIM:]
        mx_s[rows, :] = jnp.where(head0, m0, m1)

    w = jnp.concatenate([wq_ref[...], wk_ref[...], wv_ref[...]], axis=1)

    def project(chunk):
        rows = pl.ds(chunk * PROJ_ROWS, PROJ_ROWS)
        qkv = jnp.dot(h_ref[rows, :], w, preferred_element_type=F32)
        xq[rows, :] = qkv[:, :col] * (1.0 / math.sqrt(HEAD_DIM))
        xk[rows, :] = qkv[:, col:2 * col]
        xv[rows, :] = qkv[:, 2 * col:]

    general, first = _block_starts()
    blocks = [(p, start, False) for p, start in first] + [(p, start, True) for p, start in general]
    n_chunks = SEQ // PROJ_ROWS
    per_chunk = len(blocks) // n_chunks
    for chunk in range(n_chunks):
        group = blocks[chunk * per_chunk:(chunk + 1) * per_chunk]
        held = [scores(*blk) for blk in group]
        project(chunk)
        for s, blk in zip(held, group):
            attend(s, *blk)

    for r in range(4):
        for a in range(SEG4 // BLK):
            nat = pl.ds(4 * BLK * a + r, BLK, stride=4)
            r4 = r * SEG4 + a * BLK
            p1, p2 = pl.ds(SEQ + r4, BLK), pl.ds(2 * SEQ + r4, BLK)
            x0, x1, x2 = mx_s[nat, :], mx_s[p1, :], mx_s[p2, :]
            m_all = jnp.maximum(jnp.maximum(x0, x1), x2)
            w0, w1, w2 = jnp.exp(x0 - m_all), jnp.exp(x1 - m_all), jnp.exp(x2 - m_all)
            num = w0 * num_s[nat, :] + w1 * num_s[p1, :] + w2 * num_s[p2, :]
            den = w0 * den_s[nat, :] + w1 * den_s[p1, :] + w2 * den_s[p2, :]
            l4[nat, :] = num / den
    o_ref[...] = l4[...].astype(BF16)


def _attention(h, w_in, slopes):
    t = h.shape[0]
    n_hp = HEADS // 2
    n_items = (t // SEQ) * n_hp
    col = 2 * HEAD_DIM
    proj_item = lambda s: jnp.minimum(s, n_items - 1)
    attn_item = lambda s: jnp.maximum(s - 1, 0)
    w_spec = lambda kind: pl.BlockSpec(
        (D_MODEL, col),
        lambda s, kind=kind: (0, (2 + kind) * (D_MODEL // col) + proj_item(s) % n_hp))
    f32_seq = lambda: pltpu.VMEM((SEQ, col), F32)
    f32_all = lambda: pltpu.VMEM((N_PAT * SEQ, col), F32)
    return pl.pallas_call(
        _attn_kernel,
        grid=(n_items + 1,),
        in_specs=[
            pl.BlockSpec((None, 2, col), lambda s: (attn_item(s) % n_hp, 0, 0)),
            pl.BlockSpec((SEQ, D_MODEL), lambda s: (proj_item(s) // n_hp, 0)),
            w_spec(0), w_spec(1), w_spec(2),
        ],
        out_specs=pl.BlockSpec((None, SEQ, col),
                               lambda s: (attn_item(s) % n_hp, attn_item(s) // n_hp, 0)),
        out_shape=jax.ShapeDtypeStruct((HEADS // 2, t, col), BF16),
        scratch_shapes=[
            f32_seq(), f32_seq(), f32_seq(), f32_seq(),
            pltpu.VMEM((N_PAT * SEQ, col), BF16),
            pltpu.VMEM((2, col, N_PAT * SEQ), BF16),
            pltpu.VMEM((2, N_PAT * SEQ, col), BF16),
            pltpu.VMEM((2, 2 * BLK, col), BF16),
            pltpu.VMEM((N_PAT - 1, BLK, 4 * BLK), F32),
            pltpu.VMEM((N_PAT, BLK, 2 * BLK), F32),
            f32_all(), f32_all(), f32_all(),
        ],
        compiler_params=pltpu.CompilerParams(
            dimension_semantics=("arbitrary",),
            vmem_limit_bytes=VMEM_LIMIT_BYTES),
        name="dilated_attn",
    )(slopes, h, w_in, w_in, w_in)


def _merge_kernel(u_ref, v_ref, ga_ref, gb_ref, yb_ref, x_ref, ws_ref, bs_ref,
                  wa_ref, wb_ref, wo_ref, g_ref, o_ref, ya_ref):
    tm = u_ref.shape[0]
    ti = lax.broadcasted_iota(jnp.int32, (CHUNK, CHUNK), 0)
    si = lax.broadcasted_iota(jnp.int32, (CHUNK, CHUNK), 1)
    causal = si <= ti
    for g in range(GROUPS):
        w = jnp.where(causal, ws_ref[g], 0.0).astype(BF16)
        bcol = bs_ref[:, g:g + 1]
        cols = slice(g * GROUP_DIM, (g + 1) * GROUP_DIM)
        for c in range(tm // CHUNK):
            rows = slice(c * CHUNK, (c + 1) * CHUNK)
            mixed = jnp.dot(w, v_ref[rows, cols], preferred_element_type=F32) + bcol
            ya_ref[rows, cols] = (u_ref[rows, cols].astype(F32) * mixed).astype(BF16)
    a = jnp.dot(ya_ref[...], wa_ref[...], preferred_element_type=F32)
    yb = jnp.concatenate([yb_ref[hp] for hp in range(HEADS // 2)], axis=1)
    b = jnp.dot(yb, wb_ref[...], preferred_element_type=F32)
    merged = ga_ref[...].astype(F32) * a + gb_ref[...].astype(F32) * b
    y = jnp.dot(merged.astype(BF16), wo_ref[...], preferred_element_type=F32)
    o_ref[...] = x_ref[...] + _rms_norm(y, g_ref[...])


def _merge(z, yb, x2, w_s, b_s_t, w_a, w_b, w_o, g, tm):
    t = x2.shape[0]
    row = lambda c: pl.BlockSpec((tm, D_MODEL), lambda i, c=c: (i, c))
    const = lambda shape: pl.BlockSpec(shape, lambda i: (0,) * len(shape))
    return pl.pallas_call(
        _merge_kernel,
        grid=(t // tm,),
        in_specs=[row(0), row(1), row(2), row(3),
                  pl.BlockSpec((HEADS // 2, tm, 2 * HEAD_DIM), lambda i: (0, i, 0)), row(0),
                  const((GROUPS, CHUNK, CHUNK)), const((CHUNK, GROUPS)),
                  const((D_MODEL, D_MODEL)), const((D_MODEL, D_MODEL)),
                  const((D_MODEL, D_MODEL)), const((1, D_MODEL))],
        out_specs=row(0),
        out_shape=jax.ShapeDtypeStruct((t, D_MODEL), F32),
        scratch_shapes=[pltpu.VMEM((tm, D_MODEL), BF16)],
        compiler_params=pltpu.CompilerParams(
            dimension_semantics=("arbitrary",),
            vmem_limit_bytes=VMEM_LIMIT_BYTES),
        name="merge",
    )(z, z, z, z, yb, x2, w_s, b_s_t, w_a, w_b, w_o, g)


def _ffn_kernel(x_ref, gpre_ref, w1_ref, w2_ref, gpost_ref, o_ref):
    x = x_ref[...]
    h = _rms_norm(x, gpre_ref[...]).astype(BF16)
    f = jnp.dot(h, w1_ref[...], preferred_element_type=F32)
    f = jnp.square(jnp.maximum(f, 0.0)).astype(BF16)
    y = jnp.dot(f, w2_ref[...], preferred_element_type=F32)
    o_ref[...] = x + _rms_norm(y, gpost_ref[...])


def _ffn(x1, g_pre, w1, w2, g_post, tm):
    t = x1.shape[0]
    const = lambda shape, **kw: pl.BlockSpec(shape, lambda i: (0,) * len(shape), **kw)
    return pl.pallas_call(
        _ffn_kernel,
        grid=(t // tm,),
        in_specs=[
            pl.BlockSpec((tm, D_MODEL), lambda i: (i, 0)),
            const((1, D_MODEL)),
            const((D_MODEL, D_FF), pipeline_mode=pl.Buffered(1)),
            const((D_FF, D_MODEL), pipeline_mode=pl.Buffered(1)),
            const((1, D_MODEL)),
        ],
        out_specs=pl.BlockSpec((tm, D_MODEL), lambda i: (i, 0)),
        out_shape=jax.ShapeDtypeStruct((t, D_MODEL), F32),
        compiler_params=pltpu.CompilerParams(
            dimension_semantics=("arbitrary",),
            vmem_limit_bytes=VMEM_LIMIT_BYTES),
        name="ffn",
    )(x1, g_pre, w1, w2, g_post)


def kernel(x, norm_mix_pre, w_in, b_gate, ln_v_g, ln_v_b, w_s, b_s, w_a_proj, w_b_proj,
           w_out, norm_mix_post, norm_ffn_pre, w_ff1, w_ff2, norm_ffn_post):
    bsz, s, d = x.shape
    assert (s, d) == (SEQ, D_MODEL)
    depth = w_in.shape[0]
    slopes = jnp.exp2(-8.0 * jnp.arange(1, HEADS + 1, dtype=F32) / HEADS)
    slopes = jnp.broadcast_to(slopes.reshape(HEADS // 2, 2, 1), (HEADS // 2, 2, 2 * HEAD_DIM))
    x2 = x.reshape(bsz * s, d)
    for l in range(depth):
        w_in_bf = w_in[l].astype(BF16)
        z, h = _inproj(x2, norm_mix_pre[l][None], w_in_bf, ln_v_g[l][None],
                       ln_v_b[l][None], b_gate[l], tm=512)
        yb = _attention(h, w_in_bf, slopes)
        x2 = _merge(z, yb, x2, w_s[l], b_s[l].T, w_a_proj[l].astype(BF16),
                    w_b_proj[l].astype(BF16), w_out[l].astype(BF16),
                    norm_mix_post[l][None], tm=512)
        x2 = _ffn(x2, norm_ffn_pre[l][None], w_ff1[l].astype(BF16), w_ff2[l].astype(BF16),
                  norm_ffn_post[l][None], tm=512)
    return x2.reshape(bsz, s, d)
```

```python
import math

import jax
import jax.numpy as jnp
from jax import lax
from jax.experimental import pallas as pl
from jax.experimental.pallas import tpu as pltpu

D_MODEL = 1024
SEQ = 2048
CHUNK = 128
GROUPS = 8
GROUP_DIM = D_MODEL // GROUPS
HEADS = 16
HEAD_DIM = D_MODEL // HEADS
PATTERNS = ((128, 1), (512, 4), (2048, 16))
BLK = 128
D_FF = 4 * D_MODEL
EPS = 1e-6
N_IN_BLOCKS = 7
NEG = -1e30

VMEM_LIMIT_BYTES = 56 * 1024 * 1024

F32 = jnp.float32
BF16 = jnp.bfloat16


def _rms_norm(x, g):
    return x * lax.rsqrt(jnp.mean(x * x, axis=-1, keepdims=True) + EPS) * g


def _sigmoid(x):
    return 0.5 * jnp.tanh(0.5 * x) + 0.5


def _inproj_kernel(x_ref, g_ref, w_ref, lng_ref, lnb_ref, bg_ref, z_ref, zq_ref):
    h = _rms_norm(x_ref[...], g_ref[...]).astype(BF16)

    def proj(j):
        return jnp.dot(h, w_ref[:, j * D_MODEL:(j + 1) * D_MODEL], preferred_element_type=F32)

    def put(c, val):
        z_ref[:, c * D_MODEL:(c + 1) * D_MODEL] = val.astype(BF16)

    def put_heads(kind, val):
        val = val.astype(BF16)
        for hp in range(HEADS // 2):
            zq_ref[kind, hp] = val[:, hp * 2 * HEAD_DIM:(hp + 1) * 2 * HEAD_DIM]

    put(0, jax.nn.gelu(proj(0)))
    v = jax.nn.gelu(proj(1))
    mu = jnp.mean(v, axis=-1, keepdims=True)
    var = jnp.mean(jnp.square(v - mu), axis=-1, keepdims=True)
    put(1, (v - mu) * lax.rsqrt(var + EPS) * lng_ref[...] + lnb_ref[...])
    put_heads(0, proj(2) * (1.0 / math.sqrt(HEAD_DIM)))
    put_heads(1, proj(3))
    put_heads(2, proj(4))
    put(2, _sigmoid(proj(5) + bg_ref[0:1, :]))
    put(3, _sigmoid(proj(6) + bg_ref[1:2, :]))


def _inproj(x2, g, w_in, ln_g, ln_b, b_gate, tm):
    t = x2.shape[0]
    const = lambda shape, **kw: pl.BlockSpec(shape, lambda i: (0,) * len(shape), **kw)
    return pl.pallas_call(
        _inproj_kernel,
        grid=(t // tm,),
        in_specs=[
            pl.BlockSpec((tm, D_MODEL), lambda i: (i, 0)),
            const((1, D_MODEL)),
            const((D_MODEL, N_IN_BLOCKS * D_MODEL), pipeline_mode=pl.Buffered(1)),
            const((1, D_MODEL)),
            const((1, D_MODEL)),
            const((2, D_MODEL)),
        ],
        out_specs=[
            pl.BlockSpec((tm, 4 * D_MODEL), lambda i: (i, 0)),
            pl.BlockSpec((3, HEADS // 2, tm, 2 * HEAD_DIM), lambda i: (0, 0, i, 0)),
        ],
        out_shape=[
            jax.ShapeDtypeStruct((t, 4 * D_MODEL), BF16),
            jax.ShapeDtypeStruct((3, HEADS // 2, t, 2 * HEAD_DIM), BF16),
        ],
        compiler_params=pltpu.CompilerParams(
            dimension_semantics=("arbitrary",),
            vmem_limit_bytes=VMEM_LIMIT_BYTES),
        name="inproj",
    )(x2, g, w_in, ln_g, ln_b, b_gate)


N_PAT = len(PATTERNS)
SEG4 = SEQ // 4


def _attn_body(slope_ref, src_refs, o_ref, attn_set, stage_set,
               xf_refs, l4_refs, res_s, ee_s, bias_g, bias_f, num_s, den_s, mx_s):
    q_a, kh_a, vh_a = attn_set
    q_n, kh_n, vh_n = stage_set
    lane = lax.broadcasted_iota(jnp.int32, (BLK, 2 * HEAD_DIM), 1)
    head0 = lane < HEAD_DIM
    top_half = lax.broadcasted_iota(jnp.int32, (2 * HEAD_DIM, BLK), 0) < HEAD_DIM

    def emit(dst_rows, val, kind):
        if kind == 0:
            q_n[dst_rows, :] = val.astype(BF16)
        elif kind == 1:
            kt = val.T
            kh_n[0, :, dst_rows] = jnp.where(top_half, kt, 0.0).astype(BF16)
            kh_n[1, :, dst_rows] = jnp.where(top_half, 0.0, kt).astype(BF16)
        else:
            vh_n[0, dst_rows, :] = jnp.where(head0, val, 0.0).astype(BF16)
            vh_n[1, dst_rows, :] = jnp.where(head0, 0.0, val).astype(BF16)

    def stage_natural(kind, c):
        rows = pl.ds(c * BLK, BLK)
        x = src_refs[kind][rows, :].astype(F32)
        xf_refs[kind][rows, :] = x
        emit(rows, x, kind)

    def stage_dil4(kind, c):
        r, a = c // 4, c % 4
        x = xf_refs[kind][pl.ds(4 * BLK * a + r, BLK, stride=4), :]
        dst = r * SEG4 + a * BLK
        l4_refs[kind][pl.ds(dst, BLK), :] = x
        emit(pl.ds(SEQ + dst, BLK), x, kind)

    def stage_dil16(kind, c):
        x = l4_refs[kind][pl.ds((c % 4) * SEG4 + c // 4, BLK, stride=4), :]
        emit(pl.ds(2 * SEQ + c * BLK, BLK), x, kind)

    stage_tasks = [(fn, kind, c) for fn in (stage_natural, stage_dil4, stage_dil16)
                   for kind in range(3) for c in range(SEQ // BLK)]

    e0 = jnp.where(head0, 1.0, 0.0).astype(BF16)
    e1 = jnp.where(head0, 0.0, 1.0).astype(BF16)
    for i in range(2):
        ee_s[0, pl.ds(i * BLK, BLK), :] = e0
        ee_s[1, pl.ds(i * BLK, BLK), :] = e1

    qi = lax.broadcasted_iota(jnp.int32, (BLK, BLK), 0)
    kj = lax.broadcasted_iota(jnp.int32, (BLK, BLK), 1)
    d_cur = (qi - kj).astype(F32)
    ok_cur = kj <= qi
    ok_prev = kj >= qi
    for p, (_, dil) in enumerate(PATTERNS):
        for e in range(2):
            sl = slope_ref[e:e + 1, :]
            b_cur = jnp.where(ok_cur, -sl * (dil * d_cur), NEG)
            bias_f[p, :, e * BLK:(e + 1) * BLK] = b_cur
            if p < N_PAT - 1:
                b_prev = jnp.where(ok_prev, -sl * (dil * (d_cur + BLK)), NEG)
                bias_g[p, :, 2 * e * BLK:(2 * e + 1) * BLK] = b_prev
                bias_g[p, :, (2 * e + 1) * BLK:(2 * e + 2) * BLK] = b_cur

    def block(p, start, with_prev):
        nk = 2 * BLK if with_prev else BLK
        keys = pl.ds(start - BLK, nk) if with_prev else pl.ds(start, nk)
        q = q_a[pl.ds(start, BLK), :]
        kcat = jnp.concatenate([kh_a[0, :, keys], kh_a[1, :, keys]], axis=1)
        b = bias_g[p] if with_prev else bias_f[p]
        s = jnp.dot(q, kcat, preferred_element_type=F32) + b
        m0 = jnp.max(s[:, :nk], axis=-1, keepdims=True)
        m1 = jnp.max(s[:, nk:], axis=-1, keepdims=True)
        pr = jnp.concatenate([jnp.exp(s[:, :nk] - m0), jnp.exp(s[:, nk:] - m1)],
                             axis=1).astype(BF16)
        vcat = jnp.concatenate(
            [jnp.concatenate([vh_a[0, keys, :], vh_a[1, keys, :]], axis=0),
             jnp.concatenate([ee_s[0, pl.ds(0, nk), :], ee_s[1, pl.ds(0, nk), :]], axis=0)],
            axis=1)
        o = jnp.dot(pr, vcat, preferred_element_type=F32)
        if p == N_PAT - 1:
            c = (start - p * SEQ) // BLK
            rows = pl.ds(p * SEQ + (c % 4) * SEG4 + c // 4, BLK, stride=4)
        else:
            rows = pl.ds(start, BLK)
        num_s[rows, :] = o[:, :2 * HEAD_DIM]
        den_s[rows, :] = o[:, 2 * HEAD_DIM:]
        mx_s[rows, :] = jnp.where(head0, m0, m1)

    def mix(r, a):
        nat = pl.ds(4 * BLK * a + r, BLK, stride=4)
        r4 = r * SEG4 + a * BLK
        p1, p2 = pl.ds(SEQ + r4, BLK), pl.ds(2 * SEQ + r4, BLK)
        x0, x1, x2 = mx_s[nat, :], mx_s[p1, :], mx_s[p2, :]
        m_all = jnp.maximum(jnp.maximum(x0, x1), x2)
        w0, w1, w2 = jnp.exp(x0 - m_all), jnp.exp(x1 - m_all), jnp.exp(x2 - m_all)
        num = w0 * num_s[nat, :] + w1 * num_s[p1, :] + w2 * num_s[p2, :]
        den = w0 * den_s[nat, :] + w1 * den_s[p1, :] + w2 * den_s[p2, :]
        res_s[nat, :] = num / den

    blocks = []
    for p in reversed(range(N_PAT)):
        dil = PATTERNS[p][1]
        seg = SEQ // dil
        blocks += [(p, p * SEQ + r * seg + n * BLK, n > 0)
                   for r in range(dil) for n in range(seg // BLK)]
    per_block = len(stage_tasks) // len(blocks)
    n_dil1 = SEQ // BLK
    for i, blk in enumerate(blocks):
        block(*blk)
        for fn, kind, c in stage_tasks[i * per_block:(i + 1) * per_block]:
            fn(kind, c)
        n = i - (len(blocks) - n_dil1)
        if n >= 0 and n % 4 == 3:
            a = n // 4
            for r in range(4):
                mix(r, a)
            rows = pl.ds(4 * BLK * a, 4 * BLK)
            o_ref[rows, :] = res_s[rows, :].astype(BF16)


def _attn_kernel(slope_ref, q_ref, k_ref, v_ref, o_ref,
                 xq, xk, xv, lq, lk, lv, res_s, q_0, kh_0, vh_0, q_1, kh_1, vh_1,
                 ee_s, bias_g, bias_f, num_s, den_s, mx_s):
    step = pl.program_id(0)
    sets = ((q_0, kh_0, vh_0), (q_1, kh_1, vh_1))

    @pl.when(step == 0)
    def _():
        for ref in sets[1]:
            ref[...] = jnp.zeros(ref.shape, ref.dtype)

    for parity in range(2):
        @pl.when(step % 2 == parity)
        def _(parity=parity):
            _attn_body(slope_ref, (q_ref, k_ref, v_ref), o_ref,
                       sets[1 - parity], sets[parity],
                       (xq, xk, xv), (lq, lk, lv), res_s,
                       ee_s, bias_g, bias_f, num_s, den_s, mx_s)


def _attention(zq, slopes):
    t = zq.shape[2]
    n_hp = HEADS // 2
    n_items = (t // SEQ) * n_hp
    col = 2 * HEAD_DIM
    stage_item = lambda s: jnp.minimum(s, n_items - 1)
    attn_item = lambda s: jnp.maximum(s - 1, 0)
    qkv_spec = lambda kind: pl.BlockSpec(
        (None, None, SEQ, col),
        lambda s, kind=kind: (kind, stage_item(s) % n_hp, stage_item(s) // n_hp, 0))
    f32_seq = lambda: pltpu.VMEM((SEQ, col), F32)
    f32_all = lambda: pltpu.VMEM((N_PAT * SEQ, col), F32)
    staged_set = lambda: [pltpu.VMEM((N_PAT * SEQ, col), BF16),
                          pltpu.VMEM((2, col, N_PAT * SEQ), BF16),
                          pltpu.VMEM((2, N_PAT * SEQ, col), BF16)]
    return pl.pallas_call(
        _attn_kernel,
        grid=(n_items + 1,),
        in_specs=[
            pl.BlockSpec((None, 2, col), lambda s: (attn_item(s) % n_hp, 0, 0)),
            qkv_spec(0), qkv_spec(1), qkv_spec(2),
        ],
        out_specs=pl.BlockSpec((None, SEQ, col),
                               lambda s: (attn_item(s) % n_hp, attn_item(s) // n_hp, 0)),
        out_shape=jax.ShapeDtypeStruct((HEADS // 2, t, col), BF16),
        scratch_shapes=[f32_seq() for _ in range(7)]
                       + staged_set() + staged_set()
                       + [pltpu.VMEM((2, 2 * BLK, col), BF16),
                          pltpu.VMEM((N_PAT - 1, BLK, 4 * BLK), F32),
                          pltpu.VMEM((N_PAT, BLK, 2 * BLK), F32),
                          f32_all(), f32_all(), f32_all()],
        compiler_params=pltpu.CompilerParams(
            dimension_semantics=("arbitrary",),
            vmem_limit_bytes=VMEM_LIMIT_BYTES),
        name="dilated_attn",
    )(slopes, zq, zq, zq)


def _merge_kernel(u_ref, v_ref, ga_ref, gb_ref, yb_ref, x_ref, ws_ref, bs_ref,
                  wa_ref, wb_ref, wo_ref, g_ref, o_ref, ya_ref):
    tm = u_ref.shape[0]
    ti = lax.broadcasted_iota(jnp.int32, (CHUNK, CHUNK), 0)
    si = lax.broadcasted_iota(jnp.int32, (CHUNK, CHUNK), 1)
    causal = si <= ti
    for g in range(GROUPS):
        w = jnp.where(causal, ws_ref[g], 0.0).astype(BF16)
        bcol = bs_ref[:, g:g + 1]
        cols = slice(g * GROUP_DIM, (g + 1) * GROUP_DIM)
        for c in range(tm // CHUNK):
            rows = slice(c * CHUNK, (c + 1) * CHUNK)
            mixed = jnp.dot(w, v_ref[rows, cols], preferred_element_type=F32) + bcol
            ya_ref[rows, cols] = (u_ref[rows, cols].astype(F32) * mixed).astype(BF16)
    a = jnp.dot(ya_ref[...], wa_ref[...], preferred_element_type=F32)
    yb = jnp.concatenate([yb_ref[hp] for hp in range(HEADS // 2)], axis=1)
    b = jnp.dot(yb, wb_ref[...], preferred_element_type=F32)
    merged = ga_ref[...].astype(F32) * a + gb_ref[...].astype(F32) * b
    y = jnp.dot(merged.astype(BF16), wo_ref[...], preferred_element_type=F32)
    o_ref[...] = x_ref[...] + _rms_norm(y, g_ref[...])


def _merge(z, yb, x2, w_s, b_s_t, w_a, w_b, w_o, g, tm):
    t = x2.shape[0]
    row = lambda c: pl.BlockSpec((tm, D_MODEL), lambda i, c=c: (i, c))
    const = lambda shape: pl.BlockSpec(shape, lambda i: (0,) * len(shape))
    return pl.pallas_call(
        _merge_kernel,
        grid=(t // tm,),
        in_specs=[row(0), row(1), row(2), row(3),
                  pl.BlockSpec((HEADS // 2, tm, 2 * HEAD_DIM), lambda i: (0, i, 0)), row(0),
                  const((GROUPS, CHUNK, CHUNK)), const((CHUNK, GROUPS)),
                  const((D_MODEL, D_MODEL)), const((D_MODEL, D_MODEL)),
                  const((D_MODEL, D_MODEL)), const((1, D_MODEL))],
        out_specs=row(0),
        out_shape=jax.ShapeDtypeStruct((t, D_MODEL), F32),
        scratch_shapes=[pltpu.VMEM((tm, D_MODEL), BF16)],
        compiler_params=pltpu.CompilerParams(
            dimension_semantics=("arbitrary",),
            vmem_limit_bytes=VMEM_LIMIT_BYTES),
        name="merge",
    )(z, z, z, z, yb, x2, w_s, b_s_t, w_a, w_b, w_o, g)


def _ffn_kernel(x_ref, gpre_ref, w1_ref, w2_ref, gpost_ref, o_ref):
    x = x_ref[...]
    h = _rms_norm(x, gpre_ref[...]).astype(BF16)
    f = jnp.dot(h, w1_ref[...], preferred_element_type=F32)
    f = jnp.square(jnp.maximum(f, 0.0)).astype(BF16)
    y = jnp.dot(f, w2_ref[...], preferred_element_type=F32)
    o_ref[...] = x + _rms_norm(y, gpost_ref[...])


def _ffn(x1, g_pre, w1, w2, g_post, tm):
    t = x1.shape[0]
    const = lambda shape, **kw: pl.BlockSpec(shape, lambda i: (0,) * len(shape), **kw)
    return pl.pallas_call(
        _ffn_kernel,
        grid=(t // tm,),
        in_specs=[
            pl.BlockSpec((tm, D_MODEL), lambda i: (i, 0)),
            const((1, D_MODEL)),
            const((D_MODEL, D_FF), pipeline_mode=pl.Buffered(1)),
            const((D_FF, D_MODEL), pipeline_mode=pl.Buffered(1)),
            const((1, D_MODEL)),
        ],
        out_specs=pl.BlockSpec((tm, D_MODEL), lambda i: (i, 0)),
        out_shape=jax.ShapeDtypeStruct((t, D_MODEL), F32),
        compiler_params=pltpu.CompilerParams(
            dimension_semantics=("arbitrary",),
            vmem_limit_bytes=VMEM_LIMIT_BYTES),
        name="ffn",
    )(x1, g_pre, w1, w2, g_post)


def kernel(x, norm_mix_pre, w_in, b_gate, ln_v_g, ln_v_b, w_s, b_s, w_a_proj, w_b_proj,
           w_out, norm_mix_post, norm_ffn_pre, w_ff1, w_ff2, norm_ffn_post):
    bsz, s, d = x.shape
    assert (s, d) == (SEQ, D_MODEL)
    depth = w_in.shape[0]
    slopes = jnp.exp2(-8.0 * jnp.arange(1, HEADS + 1, dtype=F32) / HEADS)
    slopes = jnp.broadcast_to(slopes.reshape(HEADS // 2, 2, 1), (HEADS // 2, 2, 2 * HEAD_DIM))
    x2 = x.reshape(bsz * s, d)
    for l in range(depth):
        z, zq = _inproj(x2, norm_mix_pre[l][None], w_in[l].astype(BF16), ln_v_g[l][None],
                        ln_v_b[l][None], b_gate[l], tm=512)
        yb = _attention(zq, slopes)
        x2 = _merge(z, yb, x2, w_s[l], b_s[l].T, w_a_proj[l].astype(BF16),
                    w_b_proj[l].astype(BF16), w_out[l].astype(BF16),
                    norm_mix_post[l][None], tm=512)
        x2 = _ffn(x2, norm_ffn_pre[l][None], w_ff1[l].astype(BF16), w_ff2[l].astype(BF16),
                  norm_ffn_post[l][None], tm=512)
    return x2.reshape(bsz, s, d)
```

```python
import math

import jax
import jax.numpy as jnp
from jax import lax
from jax.experimental import pallas as pl
from jax.experimental.pallas import tpu as pltpu

D_MODEL = 1024
SEQ = 2048
CHUNK = 128
GROUPS = 8
GROUP_DIM = D_MODEL // GROUPS
HEADS = 16
HEAD_DIM = D_MODEL // HEADS
PATTERNS = ((128, 1), (512, 4), (2048, 16))
BLK = 128
D_FF = 4 * D_MODEL
EPS = 1e-6
N_IN_BLOCKS = 7
NEG = -1e30

VMEM_LIMIT_BYTES = 56 * 1024 * 1024

F32 = jnp.float32
BF16 = jnp.bfloat16


def _rms_norm(x, g):
    return x * lax.rsqrt(jnp.mean(x * x, axis=-1, keepdims=True) + EPS) * g


def _sigmoid(x):
    return 0.5 * jnp.tanh(0.5 * x) + 0.5


def _inproj_kernel(x_ref, g_ref, w_ref, lng_ref, lnb_ref, bg_ref, z_ref, zq_ref):
    h = _rms_norm(x_ref[...], g_ref[...]).astype(BF16)

    def proj(j):
        return jnp.dot(h, w_ref[:, j * D_MODEL:(j + 1) * D_MODEL], preferred_element_type=F32)

    def put(c, val):
        z_ref[:, c * D_MODEL:(c + 1) * D_MODEL] = val.astype(BF16)

    def put_heads(kind, val):
        val = val.astype(BF16)
        for hp in range(HEADS // 2):
            zq_ref[kind, hp] = val[:, hp * 2 * HEAD_DIM:(hp + 1) * 2 * HEAD_DIM]

    put(0, jax.nn.gelu(proj(0)))
    v = jax.nn.gelu(proj(1))
    mu = jnp.mean(v, axis=-1, keepdims=True)
    var = jnp.mean(jnp.square(v - mu), axis=-1, keepdims=True)
    put(1, (v - mu) * lax.rsqrt(var + EPS) * lng_ref[...] + lnb_ref[...])
    put_heads(0, proj(2) * (1.0 / math.sqrt(HEAD_DIM)))
    put_heads(1, proj(3))
    put_heads(2, proj(4))
    put(2, _sigmoid(proj(5) + bg_ref[0:1, :]))
    put(3, _sigmoid(proj(6) + bg_ref[1:2, :]))


def _inproj(x2, g, w_in, ln_g, ln_b, b_gate, tm):
    t = x2.shape[0]
    const = lambda shape, **kw: pl.BlockSpec(shape, lambda i: (0,) * len(shape), **kw)
    return pl.pallas_call(
        _inproj_kernel,
        grid=(t // tm,),
        in_specs=[
            pl.BlockSpec((tm, D_MODEL), lambda i: (i, 0)),
            const((1, D_MODEL)),
            const((D_MODEL, N_IN_BLOCKS * D_MODEL), pipeline_mode=pl.Buffered(1)),
            const((1, D_MODEL)),
            const((1, D_MODEL)),
            const((2, D_MODEL)),
        ],
        out_specs=[
            pl.BlockSpec((tm, 4 * D_MODEL), lambda i: (i, 0)),
            pl.BlockSpec((3, HEADS // 2, tm, 2 * HEAD_DIM), lambda i: (0, 0, i, 0)),
        ],
        out_shape=[
            jax.ShapeDtypeStruct((t, 4 * D_MODEL), BF16),
            jax.ShapeDtypeStruct((3, HEADS // 2, t, 2 * HEAD_DIM), BF16),
        ],
        compiler_params=pltpu.CompilerParams(
            dimension_semantics=("arbitrary",),
            vmem_limit_bytes=VMEM_LIMIT_BYTES),
        name="inproj",
    )(x2, g, w_in, ln_g, ln_b, b_gate)


N_PAT = len(PATTERNS)
SEG4 = SEQ // 4
HP_PER_STEP = 2


def _block_starts():
    general, first = [], []
    for p, (_, dil) in enumerate(PATTERNS):
        seg = SEQ // dil
        for r in range(dil):
            for n in range(seg // BLK):
                (first if n == 0 else general).append((p, p * SEQ + r * seg + n * BLK))
    return general, first


def _attn_kernel(slope_ref, q_ref, k_ref, v_ref, o_ref, *scratch):
    for i in range(HP_PER_STEP):
        _attn_head_pair(slope_ref.at[i], q_ref.at[i], k_ref.at[i], v_ref.at[i], o_ref.at[i],
                        *scratch)


def _attn_head_pair(slope_ref, q_ref, k_ref, v_ref, o_ref,
                    xf, l4, q_s, kh_s, vh_s, ee_s, bias_g, bias_f, num_s, den_s, mx_s):
    lane = lax.broadcasted_iota(jnp.int32, (BLK, 2 * HEAD_DIM), 1)
    head0 = lane < HEAD_DIM

    top_half = lax.broadcasted_iota(jnp.int32, (2 * HEAD_DIM, BLK), 0) < HEAD_DIM

    def emit(dst_rows, val, kind):
        if kind == 0:
            q_s[dst_rows, :] = val.astype(BF16)
        elif kind == 1:
            kt = val.T
            kh_s[0, :, dst_rows] = jnp.where(top_half, kt, 0.0).astype(BF16)
            kh_s[1, :, dst_rows] = jnp.where(top_half, 0.0, kt).astype(BF16)
        else:
            vh_s[0, dst_rows, :] = jnp.where(head0, val, 0.0).astype(BF16)
            vh_s[1, dst_rows, :] = jnp.where(head0, 0.0, val).astype(BF16)

    for kind, src in enumerate((q_ref, k_ref, v_ref)):
        for c in range(SEQ // BLK):
            rows = pl.ds(c * BLK, BLK)
            x = src[rows, :].astype(F32)
            xf[rows, :] = x
            emit(rows, x, kind)
        for r in range(4):
            for a in range(SEG4 // BLK):
                x = xf[pl.ds(4 * BLK * a + r, BLK, stride=4), :]
                dst = r * SEG4 + a * BLK
                l4[pl.ds(dst, BLK), :] = x
                emit(pl.ds(SEQ + dst, BLK), x, kind)
        for c in range(16):
            x = l4[pl.ds((c % 4) * SEG4 + c // 4, BLK, stride=4), :]
            emit(pl.ds(2 * SEQ + c * BLK, BLK), x, kind)

    e0 = jnp.where(head0, 1.0, 0.0).astype(BF16)
    e1 = jnp.where(head0, 0.0, 1.0).astype(BF16)
    for i in range(2):
        ee_s[0, pl.ds(i * BLK, BLK), :] = e0
        ee_s[1, pl.ds(i * BLK, BLK), :] = e1

    qi = lax.broadcasted_iota(jnp.int32, (BLK, BLK), 0)
    kj = lax.broadcasted_iota(jnp.int32, (BLK, BLK), 1)
    d_cur = (qi - kj).astype(F32)
    ok_cur = kj <= qi
    ok_prev = kj >= qi
    for p, (_, dil) in enumerate(PATTERNS):
        for e in range(2):
            sl = slope_ref[e:e + 1, :]
            b_cur = jnp.where(ok_cur, -sl * (dil * d_cur), NEG)
            bias_f[p, :, e * BLK:(e + 1) * BLK] = b_cur
            if p < N_PAT - 1:
                b_prev = jnp.where(ok_prev, -sl * (dil * (d_cur + BLK)), NEG)
                bias_g[p, :, 2 * e * BLK:(2 * e + 1) * BLK] = b_prev
                bias_g[p, :, (2 * e + 1) * BLK:(2 * e + 2) * BLK] = b_cur

    def block(p, start, with_prev):
        nk = 2 * BLK if with_prev else BLK
        keys = pl.ds(start - BLK, nk) if with_prev else pl.ds(start, nk)
        q = q_s[pl.ds(start, BLK), :]
        kcat = jnp.concatenate([kh_s[0, :, keys], kh_s[1, :, keys]], axis=1)
        b = bias_g[p] if with_prev else bias_f[p]
        s = jnp.dot(q, kcat, preferred_element_type=F32) + b
        m0 = jnp.max(s[:, :nk], axis=-1, keepdims=True)
        m1 = jnp.max(s[:, nk:], axis=-1, keepdims=True)
        pr = jnp.concatenate([jnp.exp(s[:, :nk] - m0), jnp.exp(s[:, nk:] - m1)],
                             axis=1).astype(BF16)
        vcat = jnp.concatenate(
            [jnp.concatenate([vh_s[0, keys, :], vh_s[1, keys, :]], axis=0),
             jnp.concatenate([ee_s[0, pl.ds(0, nk), :], ee_s[1, pl.ds(0, nk), :]], axis=0)],
            axis=1)
        o = jnp.dot(pr, vcat, preferred_element_type=F32)
        if p == N_PAT - 1:
            c = (start - p * SEQ) // BLK
            rows = pl.ds(p * SEQ + (c % 4) * SEG4 + c // 4, BLK, stride=4)
        else:
            rows = pl.ds(start, BLK)
        num_s[rows, :] = o[:, :2 * HEAD_DIM]
        den_s[rows, :] = o[:, 2 * HEAD_DIM:]
        mx_s[rows, :] = jnp.where(head0, m0, m1)

    general, first = _block_starts()
    for p, start in first:
        block(p, start, False)
    for p, start in general:
        block(p, start, True)

    for r in range(4):
        for a in range(SEG4 // BLK):
            nat = pl.ds(4 * BLK * a + r, BLK, stride=4)
            r4 = r * SEG4 + a * BLK
            p1, p2 = pl.ds(SEQ + r4, BLK), pl.ds(2 * SEQ + r4, BLK)
            x0, x1, x2 = mx_s[nat, :], mx_s[p1, :], mx_s[p2, :]
            m_all = jnp.maximum(jnp.maximum(x0, x1), x2)
            w0, w1, w2 = jnp.exp(x0 - m_all), jnp.exp(x1 - m_all), jnp.exp(x2 - m_all)
            num = w0 * num_s[nat, :] + w1 * num_s[p1, :] + w2 * num_s[p2, :]
            den = w0 * den_s[nat, :] + w1 * den_s[p1, :] + w2 * den_s[p2, :]
            xf[nat, :] = num / den
    o_ref[...] = xf[...].astype(BF16)


def _attention(zq, slopes):
    t = zq.shape[2]
    nb = t // SEQ
    col = 2 * HEAD_DIM
    qkv_spec = lambda kind: pl.BlockSpec((None, HP_PER_STEP, SEQ, col),
                                         lambda b, h, kind=kind: (kind, h, b, 0))
    f32_seq = lambda: pltpu.VMEM((SEQ, col), F32)
    f32_all = lambda: pltpu.VMEM((N_PAT * SEQ, col), F32)
    return pl.pallas_call(
        _attn_kernel,
        grid=(nb, HEADS // 2 // HP_PER_STEP),
        in_specs=[
            pl.BlockSpec((HP_PER_STEP, 2, col), lambda b, h: (h, 0, 0)),
            qkv_spec(0), qkv_spec(1), qkv_spec(2),
        ],
        out_specs=pl.BlockSpec((HP_PER_STEP, SEQ, col), lambda b, h: (h, b, 0)),
        out_shape=jax.ShapeDtypeStruct((HEADS // 2, t, col), BF16),
        scratch_shapes=[
            f32_seq(), f32_seq(),
            pltpu.VMEM((N_PAT * SEQ, col), BF16),
            pltpu.VMEM((2, col, N_PAT * SEQ), BF16),
            pltpu.VMEM((2, N_PAT * SEQ, col), BF16),
            pltpu.VMEM((2, 2 * BLK, col), BF16),
            pltpu.VMEM((N_PAT - 1, BLK, 4 * BLK), F32),
            pltpu.VMEM((N_PAT, BLK, 2 * BLK), F32),
            f32_all(), f32_all(), f32_all(),
        ],
        compiler_params=pltpu.CompilerParams(
            dimension_semantics=("arbitrary", "arbitrary"),
            vmem_limit_bytes=VMEM_LIMIT_BYTES),
        name="dilated_attn",
    )(slopes, zq, zq, zq)


def _merge_kernel(u_ref, v_ref, ga_ref, gb_ref, yb_ref, x_ref, ws_ref, bs_ref,
                  wa_ref, wb_ref, wo_ref, g_ref, o_ref, ya_ref):
    tm = u_ref.shape[0]
    ti = lax.broadcasted_iota(jnp.int32, (CHUNK, CHUNK), 0)
    si = lax.broadcasted_iota(jnp.int32, (CHUNK, CHUNK), 1)
    causal = si <= ti
    for g in range(GROUPS):
        w = jnp.where(causal, ws_ref[g], 0.0).astype(BF16)
        bcol = bs_ref[:, g:g + 1]
        cols = slice(g * GROUP_DIM, (g + 1) * GROUP_DIM)
        for c in range(tm // CHUNK):
            rows = slice(c * CHUNK, (c + 1) * CHUNK)
            mixed = jnp.dot(w, v_ref[rows, cols], preferred_element_type=F32) + bcol
            ya_ref[rows, cols] = (u_ref[rows, cols].astype(F32) * mixed).astype(BF16)
    a = jnp.dot(ya_ref[...], wa_ref[...].astype(BF16), preferred_element_type=F32)
    yb = jnp.concatenate([yb_ref[hp] for hp in range(HEADS // 2)], axis=1)
    b = jnp.dot(yb, wb_ref[...].astype(BF16), preferred_element_type=F32)
    merged = ga_ref[...].astype(F32) * a + gb_ref[...].astype(F32) * b
    y = jnp.dot(merged.astype(BF16), wo_ref[...].astype(BF16), preferred_element_type=F32)
    o_ref[...] = x_ref[...] + _rms_norm(y, g_ref[...])


def _merge(z, yb, x2, w_s, b_s_t, w_a, w_b, w_o, g, tm):
    t = x2.shape[0]
    row = lambda c: pl.BlockSpec((tm, D_MODEL), lambda i, c=c: (i, c))
    const = lambda shape, **kw: pl.BlockSpec(shape, lambda i: (0,) * len(shape), **kw)
    weight = lambda: const((D_MODEL, D_MODEL), pipeline_mode=pl.Buffered(1))
    return pl.pallas_call(
        _merge_kernel,
        grid=(t // tm,),
        in_specs=[row(0), row(1), row(2), row(3),
                  pl.BlockSpec((HEADS // 2, tm, 2 * HEAD_DIM), lambda i: (0, i, 0)), row(0),
                  const((GROUPS, CHUNK, CHUNK)), const((CHUNK, GROUPS)),
                  weight(), weight(), weight(), const((1, D_MODEL))],
        out_specs=row(0),
        out_shape=jax.ShapeDtypeStruct((t, D_MODEL), F32),
        scratch_shapes=[pltpu.VMEM((tm, D_MODEL), BF16)],
        compiler_params=pltpu.CompilerParams(
            dimension_semantics=("arbitrary",),
            vmem_limit_bytes=VMEM_LIMIT_BYTES),
        name="merge",
    )(z, z, z, z, yb, x2, w_s, b_s_t, w_a, w_b, w_o, g)


def _ffn_kernel(x_ref, gpre_ref, w1_ref, w2_ref, gpost_ref, o_ref):
    x = x_ref[...]
    h = _rms_norm(x, gpre_ref[...]).astype(BF16)
    f = jnp.dot(h, w1_ref[...], preferred_element_type=F32)
    f = jnp.square(jnp.maximum(f, 0.0)).astype(BF16)
    y = jnp.dot(f, w2_ref[...], preferred_element_type=F32)
    o_ref[...] = x + _rms_norm(y, gpost_ref[...])


def _ffn(x1, g_pre, w1, w2, g_post, tm):
    t = x1.shape[0]
    const = lambda shape, **kw: pl.BlockSpec(shape, lambda i: (0,) * len(shape), **kw)
    return pl.pallas_call(
        _ffn_kernel,
        grid=(t // tm,),
        in_specs=[
            pl.BlockSpec((tm, D_MODEL), lambda i: (i, 0)),
            const((1, D_MODEL)),
            const((D_MODEL, D_FF), pipeline_mode=pl.Buffered(1)),
            const((D_FF, D_MODEL), pipeline_mode=pl.Buffered(1)),
            const((1, D_MODEL)),
        ],
        out_specs=pl.BlockSpec((tm, D_MODEL), lambda i: (i, 0)),
        out_shape=jax.ShapeDtypeStruct((t, D_MODEL), F32),
        compiler_params=pltpu.CompilerParams(
            dimension_semantics=("arbitrary",),
            vmem_limit_bytes=VMEM_LIMIT_BYTES),
        name="ffn",
    )(x1, g_pre, w1, w2, g_post)


def kernel(x, norm_mix_pre, w_in, b_gate, ln_v_g, ln_v_b, w_s, b_s, w_a_proj, w_b_proj,
           w_out, norm_mix_post, norm_ffn_pre, w_ff1, w_ff2, norm_ffn_post):
    bsz, s, d = x.shape
    assert (s, d) == (SEQ, D_MODEL)
    depth = w_in.shape[0]
    slopes = jnp.exp2(-8.0 * jnp.arange(1, HEADS + 1, dtype=F32) / HEADS)
    slopes = jnp.broadcast_to(slopes.reshape(HEADS // 2, 2, 1), (HEADS // 2, 2, 2 * HEAD_DIM))
    x2 = x.reshape(bsz * s, d)
    for l in range(depth):
        z, zq = _inproj(x2, norm_mix_pre[l][None], w_in[l].astype(BF16), ln_v_g[l][None],
                        ln_v_b[l][None], b_gate[l], tm=512)
        yb = _attention(zq, slopes)
        x2 = _merge(z, yb, x2, w_s[l], b_s[l].T, w_a_proj[l], w_b_proj[l], w_out[l],
                    norm_mix_post[l][None], tm=512)
        x2 = _ffn(x2, norm_ffn_pre[l][None], w_ff1[l].astype(BF16), w_ff2[l].astype(BF16),
                  norm_ffn_post[l][None], tm=512)
    return x2.reshape(bsz, s, d)
```

```python
import math

import jax
import jax.numpy as jnp
from jax import lax
from jax.experimental import pallas as pl
from jax.experimental.pallas import tpu as pltpu

D_MODEL = 1024
SEQ = 2048
CHUNK = 128
GROUPS = 8
GROUP_DIM = D_MODEL // GROUPS
HEADS = 16
HEAD_DIM = D_MODEL // HEADS
PATTERNS = ((128, 1), (512, 4), (2048, 16))
BLK = 128
D_FF = 4 * D_MODEL
EPS = 1e-6
N_IN_BLOCKS = 7
NEG = -1e30
LOG2E = math.log2(math.e)
Q_SCALE = LOG2E / math.sqrt(HEAD_DIM)

VMEM_LIMIT_BYTES = 56 * 1024 * 1024

F32 = jnp.float32
BF16 = jnp.bfloat16


def _rms_norm(x, g):
    return x * lax.rsqrt(jnp.mean(x * x, axis=-1, keepdims=True) + EPS) * g


def _sigmoid(x):
    return 0.5 * jnp.tanh(0.5 * x) + 0.5


def _inproj_kernel(x_ref, g_ref, w_ref, lng_ref, lnb_ref, bg_ref, z_ref, zq_ref):
    h = _rms_norm(x_ref[...], g_ref[...]).astype(BF16)

    def proj(j):
        return jnp.dot(h, w_ref[:, j * D_MODEL:(j + 1) * D_MODEL], preferred_element_type=F32)

    def put(c, val):
        z_ref[:, c * D_MODEL:(c + 1) * D_MODEL] = val.astype(BF16)

    def put_heads(kind, val):
        val = val.astype(BF16)
        for hp in range(HEADS // 2):
            zq_ref[kind, hp] = val[:, hp * 2 * HEAD_DIM:(hp + 1) * 2 * HEAD_DIM]

    put(0, jax.nn.gelu(proj(0)))
    v = jax.nn.gelu(proj(1))
    mu = jnp.mean(v, axis=-1, keepdims=True)
    var = jnp.mean(jnp.square(v - mu), axis=-1, keepdims=True)
    put(1, (v - mu) * lax.rsqrt(var + EPS) * lng_ref[...] + lnb_ref[...])
    put(2, _sigmoid(proj(5) + bg_ref[0:1, :]))
    put(3, _sigmoid(proj(6) + bg_ref[1:2, :]))
    put_heads(0, proj(2) * Q_SCALE)
    put_heads(1, proj(3))
    put_heads(2, proj(4))


def _inproj(x2, g, w_in, ln_g, ln_b, b_gate, tm):
    t = x2.shape[0]
    const = lambda shape, **kw: pl.BlockSpec(shape, lambda i: (0,) * len(shape), **kw)
    return pl.pallas_call(
        _inproj_kernel,
        grid=(t // tm,),
        in_specs=[
            pl.BlockSpec((tm, D_MODEL), lambda i: (i, 0)),
            const((1, D_MODEL)),
            const((D_MODEL, N_IN_BLOCKS * D_MODEL), pipeline_mode=pl.Buffered(1)),
            const((1, D_MODEL)),
            const((1, D_MODEL)),
            const((2, D_MODEL)),
        ],
        out_specs=[
            pl.BlockSpec((tm, 4 * D_MODEL), lambda i: (i, 0)),
            pl.BlockSpec((3, HEADS // 2, tm, 2 * HEAD_DIM), lambda i: (0, 0, i, 0)),
        ],
        out_shape=[
            jax.ShapeDtypeStruct((t, 4 * D_MODEL), BF16),
            jax.ShapeDtypeStruct((3, HEADS // 2, t, 2 * HEAD_DIM), BF16),
        ],
        compiler_params=pltpu.CompilerParams(
            dimension_semantics=("arbitrary",),
            vmem_limit_bytes=VMEM_LIMIT_BYTES),
        name="inproj",
    )(x2, g, w_in, ln_g, ln_b, b_gate)


N_PAT = len(PATTERNS)
SEG4 = SEQ // 4
HP_PER_STEP = 2


def _block_starts():
    general, first = [], []
    for p, (_, dil) in enumerate(PATTERNS):
        seg = SEQ // dil
        for r in range(dil):
            for n in range(seg // BLK):
                (first if n == 0 else general).append((p, p * SEQ + r * seg + n * BLK))
    return general, first


def _attn_kernel(slope_ref, q_ref, k_ref, v_ref, o_ref, *scratch):
    for i in range(HP_PER_STEP):
        _attn_head_pair(slope_ref.at[i], q_ref.at[i], k_ref.at[i], v_ref.at[i], o_ref.at[i],
                        *scratch)


def _attn_head_pair(slope_ref, q_ref, k_ref, v_ref, o_ref,
                    xf, l4, q_s, kh_s, vh_s, ee_s, bias_g, bias_f, num_s, den_s, mx_s):
    lane = lax.broadcasted_iota(jnp.int32, (BLK, 2 * HEAD_DIM), 1)
    head0 = lane < HEAD_DIM

    top_half = lax.broadcasted_iota(jnp.int32, (2 * HEAD_DIM, BLK), 0) < HEAD_DIM

    def emit(dst_rows, val, kind):
        if kind == 0:
            q_s[dst_rows, :] = val.astype(BF16)
        elif kind == 1:
            kt = val.T
            kh_s[0, :, dst_rows] = jnp.where(top_half, kt, 0.0).astype(BF16)
            kh_s[1, :, dst_rows] = jnp.where(top_half, 0.0, kt).astype(BF16)
        else:
            vh_s[0, dst_rows, :] = jnp.where(head0, val, 0.0).astype(BF16)
            vh_s[1, dst_rows, :] = jnp.where(head0, 0.0, val).astype(BF16)

    for kind, src in enumerate((q_ref, k_ref, v_ref)):
        for c in range(SEQ // BLK):
            rows = pl.ds(c * BLK, BLK)
            x = src[rows, :].astype(F32)
            xf[rows, :] = x
            emit(rows, x, kind)
        for r in range(4):
            for a in range(SEG4 // BLK):
                x = xf[pl.ds(4 * BLK * a + r, BLK, stride=4), :]
                dst = r * SEG4 + a * BLK
                l4[pl.ds(dst, BLK), :] = x
                emit(pl.ds(SEQ + dst, BLK), x, kind)
        for c in range(16):
            x = l4[pl.ds((c % 4) * SEG4 + c // 4, BLK, stride=4), :]
            emit(pl.ds(2 * SEQ + c * BLK, BLK), x, kind)

    e0 = jnp.where(head0, 1.0, 0.0).astype(BF16)
    e1 = jnp.where(head0, 0.0, 1.0).astype(BF16)
    for i in range(2):
        ee_s[0, pl.ds(i * BLK, BLK), :] = e0
        ee_s[1, pl.ds(i * BLK, BLK), :] = e1

    qi = lax.broadcasted_iota(jnp.int32, (BLK, BLK), 0)
    kj = lax.broadcasted_iota(jnp.int32, (BLK, BLK), 1)
    d_cur = (qi - kj).astype(F32)
    ok_cur = kj <= qi
    ok_prev = kj >= qi
    for p, (_, dil) in enumerate(PATTERNS):
        for e in range(2):
            sl = slope_ref[e:e + 1, :]
            sl = sl * LOG2E
            b_cur = jnp.where(ok_cur, -sl * (dil * d_cur), NEG)
            bias_f[p, :, e * BLK:(e + 1) * BLK] = b_cur
            if p < N_PAT - 1:
                b_prev = jnp.where(ok_prev, -sl * (dil * (d_cur + BLK)), NEG)
                bias_g[p, :, 2 * e * BLK:(2 * e + 1) * BLK] = b_prev
                bias_g[p, :, (2 * e + 1) * BLK:(2 * e + 2) * BLK] = b_cur

    def block(p, start, with_prev):
        nk = 2 * BLK if with_prev else BLK
        keys = pl.ds(start - BLK, nk) if with_prev else pl.ds(start, nk)
        q = q_s[pl.ds(start, BLK), :]
        kcat = jnp.concatenate([kh_s[0, :, keys], kh_s[1, :, keys]], axis=1)
        b = bias_g[p] if with_prev else bias_f[p]
        s = jnp.dot(q, kcat, preferred_element_type=F32) + b
        m0 = jnp.max(s[:, :nk], axis=-1, keepdims=True)
        m1 = jnp.max(s[:, nk:], axis=-1, keepdims=True)
        pr = jnp.concatenate([jnp.exp2(s[:, :nk] - m0), jnp.exp2(s[:, nk:] - m1)],
                             axis=1).astype(BF16)
        vcat = jnp.concatenate(
            [jnp.concatenate([vh_s[0, keys, :], vh_s[1, keys, :]], axis=0),
             jnp.concatenate([ee_s[0, pl.ds(0, nk), :], ee_s[1, pl.ds(0, nk), :]], axis=0)],
            axis=1)
        o = jnp.dot(pr, vcat, preferred_element_type=F32)
        if p == N_PAT - 1:
            c = (start - p * SEQ) // BLK
            rows = pl.ds(p * SEQ + (c % 4) * SEG4 + c // 4, BLK, stride=4)
        else:
            rows = pl.ds(start, BLK)
        num_s[rows, :] = o[:, :2 * HEAD_DIM]
        den_s[rows, :] = o[:, 2 * HEAD_DIM:]
        mx_s[rows, :] = jnp.where(head0, m0, m1)

    general, first = _block_starts()
    for p, start in first:
        block(p, start, False)
    for p, start in general:
        block(p, start, True)

    for r in range(4):
        for a in range(SEG4 // BLK):
            nat = pl.ds(4 * BLK * a + r, BLK, stride=4)
            r4 = r * SEG4 + a * BLK
            p1, p2 = pl.ds(SEQ + r4, BLK), pl.ds(2 * SEQ + r4, BLK)
            x0, x1, x2 = mx_s[nat, :], mx_s[p1, :], mx_s[p2, :]
            m_all = jnp.maximum(jnp.maximum(x0, x1), x2)
            w0, w1, w2 = jnp.exp2(x0 - m_all), jnp.exp2(x1 - m_all), jnp.exp2(x2 - m_all)
            num = w0 * num_s[nat, :] + w1 * num_s[p1, :] + w2 * num_s[p2, :]
            den = w0 * den_s[nat, :] + w1 * den_s[p1, :] + w2 * den_s[p2, :]
            xf[nat, :] = num / den
    o_ref[...] = xf[...].astype(BF16)


def _attention(zq, slopes):
    t = zq.shape[2]
    nb = t // SEQ
    col = 2 * HEAD_DIM
    qkv_spec = lambda kind: pl.BlockSpec((None, HP_PER_STEP, SEQ, col),
                                         lambda b, h, kind=kind: (kind, h, b, 0))
    f32_seq = lambda: pltpu.VMEM((SEQ, col), F32)
    f32_all = lambda: pltpu.VMEM((N_PAT * SEQ, col), F32)
    return pl.pallas_call(
        _attn_kernel,
        grid=(nb, HEADS // 2 // HP_PER_STEP),
        in_specs=[
            pl.BlockSpec((HP_PER_STEP, 2, col), lambda b, h: (h, 0, 0)),
            qkv_spec(0), qkv_spec(1), qkv_spec(2),
        ],
        out_specs=pl.BlockSpec((HP_PER_STEP, SEQ, col), lambda b, h: (h, b, 0)),
        out_shape=jax.ShapeDtypeStruct((HEADS // 2, t, col), BF16),
        scratch_shapes=[
            f32_seq(), f32_seq(),
            pltpu.VMEM((N_PAT * SEQ, col), BF16),
            pltpu.VMEM((2, col, N_PAT * SEQ), BF16),
            pltpu.VMEM((2, N_PAT * SEQ, col), BF16),
            pltpu.VMEM((2, 2 * BLK, col), BF16),
            pltpu.VMEM((N_PAT - 1, BLK, 4 * BLK), F32),
            pltpu.VMEM((N_PAT, BLK, 2 * BLK), F32),
            f32_all(), f32_all(), f32_all(),
        ],
        compiler_params=pltpu.CompilerParams(
            dimension_semantics=("arbitrary", "arbitrary"),
            vmem_limit_bytes=VMEM_LIMIT_BYTES),
        name="dilated_attn",
    )(slopes, zq, zq, zq)


def _merge_kernel(u_ref, v_ref, ga_ref, gb_ref, yb_ref, x_ref, ws_ref, bs_ref,
                  wa_ref, wb_ref, wo_ref, g_ref, o_ref, ya_ref):
    tm = u_ref.shape[0]
    ti = lax.broadcasted_iota(jnp.int32, (CHUNK, CHUNK), 0)
    si = lax.broadcasted_iota(jnp.int32, (CHUNK, CHUNK), 1)
    causal = si <= ti
    for g in range(GROUPS):
        w = jnp.where(causal, ws_ref[g], 0.0).astype(BF16)
        bcol = bs_ref[:, g:g + 1]
        cols = slice(g * GROUP_DIM, (g + 1) * GROUP_DIM)
        for c in range(tm // CHUNK):
            rows = slice(c * CHUNK, (c + 1) * CHUNK)
            mixed = jnp.dot(w, v_ref[rows, cols], preferred_element_type=F32) + bcol
            ya_ref[rows, cols] = (u_ref[rows, cols].astype(F32) * mixed).astype(BF16)
    a = jnp.dot(ya_ref[...], wa_ref[...].astype(BF16), preferred_element_type=F32)
    yb = jnp.concatenate([yb_ref[hp] for hp in range(HEADS // 2)], axis=1)
    b = jnp.dot(yb, wb_ref[...].astype(BF16), preferred_element_type=F32)
    merged = ga_ref[...].astype(F32) * a + gb_ref[...].astype(F32) * b
    y = jnp.dot(merged.astype(BF16), wo_ref[...].astype(BF16), preferred_element_type=F32)
    o_ref[...] = x_ref[...] + _rms_norm(y, g_ref[...])


def _merge(z, yb, x2, w_s, b_s_t, w_a, w_b, w_o, g, tm):
    t = x2.shape[0]
    row = lambda c: pl.BlockSpec((tm, D_MODEL), lambda i, c=c: (i, c))
    const = lambda shape, **kw: pl.BlockSpec(shape, lambda i: (0,) * len(shape), **kw)
    weight = lambda: const((D_MODEL, D_MODEL), pipeline_mode=pl.Buffered(1))
    return pl.pallas_call(
        _merge_kernel,
        grid=(t // tm,),
        in_specs=[row(0), row(1), row(2), row(3),
                  pl.BlockSpec((HEADS // 2, tm, 2 * HEAD_DIM), lambda i: (0, i, 0)), row(0),
                  const((GROUPS, CHUNK, CHUNK)), const((CHUNK, GROUPS)),
                  weight(), weight(), weight(), const((1, D_MODEL))],
        out_specs=row(0),
        out_shape=jax.ShapeDtypeStruct((t, D_MODEL), F32),
        scratch_shapes=[pltpu.VMEM((tm, D_MODEL), BF16)],
        compiler_params=pltpu.CompilerParams(
            dimension_semantics=("arbitrary",),
            vmem_limit_bytes=VMEM_LIMIT_BYTES),
        name="merge",
    )(z, z, z, z, yb, x2, w_s, b_s_t, w_a, w_b, w_o, g)


def _ffn_kernel(x_ref, gpre_ref, w1_ref, w2_ref, gpost_ref, o_ref):
    x = x_ref[...]
    h = _rms_norm(x, gpre_ref[...]).astype(BF16)
    f = jnp.dot(h, w1_ref[...], preferred_element_type=F32)
    f = jnp.square(jnp.maximum(f, 0.0)).astype(BF16)
    y = jnp.dot(f, w2_ref[...], preferred_element_type=F32)
    o_ref[...] = x + _rms_norm(y, gpost_ref[...])


def _ffn(x1, g_pre, w1, w2, g_post, tm):
    t = x1.shape[0]
    const = lambda shape, **kw: pl.BlockSpec(shape, lambda i: (0,) * len(shape), **kw)
    return pl.pallas_call(
        _ffn_kernel,
        grid=(t // tm,),
        in_specs=[
            pl.BlockSpec((tm, D_MODEL), lambda i: (i, 0)),
            const((1, D_MODEL)),
            const((D_MODEL, D_FF), pipeline_mode=pl.Buffered(1)),
            const((D_FF, D_MODEL), pipeline_mode=pl.Buffered(1)),
            const((1, D_MODEL)),
        ],
        out_specs=pl.BlockSpec((tm, D_MODEL), lambda i: (i, 0)),
        out_shape=jax.ShapeDtypeStruct((t, D_MODEL), F32),
        compiler_params=pltpu.CompilerParams(
            dimension_semantics=("arbitrary",),
            vmem_limit_bytes=VMEM_LIMIT_BYTES),
        name="ffn",
    )(x1, g_pre, w1, w2, g_post)


def kernel(x, norm_mix_pre, w_in, b_gate, ln_v_g, ln_v_b, w_s, b_s, w_a_proj, w_b_proj,
           w_out, norm_mix_post, norm_ffn_pre, w_ff1, w_ff2, norm_ffn_post):
    bsz, s, d = x.shape
    assert (s, d) == (SEQ, D_MODEL)
    depth = w_in.shape[0]
    slopes = jnp.exp2(-8.0 * jnp.arange(1, HEADS + 1, dtype=F32) / HEADS)
    slopes = jnp.broadcast_to(slopes.reshape(HEADS // 2, 2, 1), (HEADS // 2, 2, 2 * HEAD_DIM))
    x2 = x.reshape(bsz * s, d)
    for l in range(depth):
        z, zq = _inproj(x2, norm_mix_pre[l][None], w_in[l].astype(BF16), ln_v_g[l][None],
                        ln_v_b[l][None], b_gate[l], tm=512)
        yb = _attention(zq, slopes)
        x2 = _merge(z, yb, x2, w_s[l], b_s[l].T, w_a_proj[l], w_b_proj[l], w_out[l],
                    norm_mix_post[l][None], tm=512)
        x2 = _ffn(x2, norm_ffn_pre[l][None], w_ff1[l].astype(BF16), w_ff2[l].astype(BF16),
                  norm_ffn_post[l][None], tm=512)
    return x2.reshape(bsz, s, d)
```

```python
import math

import jax
import jax.numpy as jnp
from jax import lax
from jax.experimental import pallas as pl
from jax.experimental.pallas import tpu as pltpu

D_MODEL = 1024
SEQ = 2048
CHUNK = 128
GROUPS = 8
GROUP_DIM = D_MODEL // GROUPS
HEADS = 16
HEAD_DIM = D_MODEL // HEADS
PATTERNS = ((128, 1), (512, 4), (2048, 16))
BLK = 128
D_FF = 4 * D_MODEL
EPS = 1e-6
N_IN_BLOCKS = 7
NEG = -1e30
LOG2E = math.log2(math.e)
Q_SCALE = LOG2E / math.sqrt(HEAD_DIM)

VMEM_LIMIT_BYTES = 56 * 1024 * 1024

F32 = jnp.float32
BF16 = jnp.bfloat16


def _rms_norm(x, g):
    return x * lax.rsqrt(jnp.mean(x * x, axis=-1, keepdims=True) + EPS) * g


def _sigmoid(x):
    return 0.5 * jnp.tanh(0.5 * x) + 0.5


def _inproj_kernel(x_ref, g_ref, w_ref, lng_ref, lnb_ref, bg_ref, z_ref, zq_ref):
    h = _rms_norm(x_ref[...], g_ref[...]).astype(BF16)

    def proj(j):
        return jnp.dot(h, w_ref[:, j * D_MODEL:(j + 1) * D_MODEL], preferred_element_type=F32)

    def put(c, val):
        z_ref[:, c * D_MODEL:(c + 1) * D_MODEL] = val.astype(BF16)

    def put_heads(kind, val):
        val = val.astype(BF16)
        for hp in range(HEADS // 2):
            zq_ref[kind, hp] = val[:, hp * 2 * HEAD_DIM:(hp + 1) * 2 * HEAD_DIM]

    put(0, jax.nn.gelu(proj(0)))
    v = jax.nn.gelu(proj(1))
    mu = jnp.mean(v, axis=-1, keepdims=True)
    var = jnp.mean(jnp.square(v - mu), axis=-1, keepdims=True)
    put(1, (v - mu) * lax.rsqrt(var + EPS) * lng_ref[...] + lnb_ref[...])
    put(2, _sigmoid(proj(5) + bg_ref[0:1, :]))
    put(3, _sigmoid(proj(6) + bg_ref[1:2, :]))
    put_heads(0, proj(2) * Q_SCALE)
    put_heads(1, proj(3))
    put_heads(2, proj(4))


def _inproj(x2, g, w_in, ln_g, ln_b, b_gate, tm):
    t = x2.shape[0]
    const = lambda shape, **kw: pl.BlockSpec(shape, lambda i: (0,) * len(shape), **kw)
    return pl.pallas_call(
        _inproj_kernel,
        grid=(t // tm,),
        in_specs=[
            pl.BlockSpec((tm, D_MODEL), lambda i: (i, 0)),
            const((1, D_MODEL)),
            const((D_MODEL, N_IN_BLOCKS * D_MODEL), pipeline_mode=pl.Buffered(1)),
            const((1, D_MODEL)),
            const((1, D_MODEL)),
            const((2, D_MODEL)),
        ],
        out_specs=[
            pl.BlockSpec((tm, 4 * D_MODEL), lambda i: (i, 0)),
            pl.BlockSpec((3, HEADS // 2, tm, 2 * HEAD_DIM), lambda i: (0, 0, i, 0)),
        ],
        out_shape=[
            jax.ShapeDtypeStruct((t, 4 * D_MODEL), BF16),
            jax.ShapeDtypeStruct((3, HEADS // 2, t, 2 * HEAD_DIM), BF16),
        ],
        compiler_params=pltpu.CompilerParams(
            dimension_semantics=("arbitrary",),
            vmem_limit_bytes=VMEM_LIMIT_BYTES),
        name="inproj",
    )(x2, g, w_in, ln_g, ln_b, b_gate)


N_PAT = len(PATTERNS)
SEG4 = SEQ // 4
HP_PER_STEP = 4


def _block_starts():
    general, first = [], []
    for p, (_, dil) in enumerate(PATTERNS):
        seg = SEQ // dil
        for r in range(dil):
            for n in range(seg // BLK):
                (first if n == 0 else general).append((p, p * SEQ + r * seg + n * BLK))
    return general, first


def _attn_kernel(slope_ref, q_ref, k_ref, v_ref, o_ref, *scratch):
    for i in range(HP_PER_STEP):
        _attn_head_pair(slope_ref.at[i], q_ref.at[i], k_ref.at[i], v_ref.at[i], o_ref.at[i],
                        *scratch)


def _attn_head_pair(slope_ref, q_ref, k_ref, v_ref, o_ref,
                    xf, l4, q_s, kh_s, vh_s, ee_s, bias_g, bias_f, num_s, den_s, mx_s):
    lane = lax.broadcasted_iota(jnp.int32, (BLK, 2 * HEAD_DIM), 1)
    head0 = lane < HEAD_DIM

    top_half = lax.broadcasted_iota(jnp.int32, (2 * HEAD_DIM, BLK), 0) < HEAD_DIM

    def emit(dst_rows, val, kind):
        if kind == 0:
            q_s[dst_rows, :] = val.astype(BF16)
        elif kind == 1:
            kt = val.T
            kh_s[0, :, dst_rows] = jnp.where(top_half, kt, 0.0).astype(BF16)
            kh_s[1, :, dst_rows] = jnp.where(top_half, 0.0, kt).astype(BF16)
        else:
            vh_s[0, dst_rows, :] = jnp.where(head0, val, 0.0).astype(BF16)
            vh_s[1, dst_rows, :] = jnp.where(head0, 0.0, val).astype(BF16)

    for kind, src in enumerate((q_ref, k_ref, v_ref)):
        for c in range(SEQ // BLK):
            rows = pl.ds(c * BLK, BLK)
            x = src[rows, :].astype(F32)
            xf[rows, :] = x
            emit(rows, x, kind)
        for r in range(4):
            for a in range(SEG4 // BLK):
                x = xf[pl.ds(4 * BLK * a + r, BLK, stride=4), :]
                dst = r * SEG4 + a * BLK
                l4[pl.ds(dst, BLK), :] = x
                emit(pl.ds(SEQ + dst, BLK), x, kind)
        for c in range(16):
            x = l4[pl.ds((c % 4) * SEG4 + c // 4, BLK, stride=4), :]
            emit(pl.ds(2 * SEQ + c * BLK, BLK), x, kind)

    e0 = jnp.where(head0, 1.0, 0.0).astype(BF16)
    e1 = jnp.where(head0, 0.0, 1.0).astype(BF16)
    for i in range(2):
        ee_s[0, pl.ds(i * BLK, BLK), :] = e0
        ee_s[1, pl.ds(i * BLK, BLK), :] = e1

    qi = lax.broadcasted_iota(jnp.int32, (BLK, BLK), 0)
    kj = lax.broadcasted_iota(jnp.int32, (BLK, BLK), 1)
    d_cur = (qi - kj).astype(F32)
    ok_cur = kj <= qi
    ok_prev = kj >= qi
    for p, (_, dil) in enumerate(PATTERNS):
        for e in range(2):
            sl = slope_ref[e:e + 1, :]
            sl = sl * LOG2E
            b_cur = jnp.where(ok_cur, -sl * (dil * d_cur), NEG)
            bias_f[p, :, e * BLK:(e + 1) * BLK] = b_cur
            if p < N_PAT - 1:
                b_prev = jnp.where(ok_prev, -sl * (dil * (d_cur + BLK)), NEG)
                bias_g[p, :, 2 * e * BLK:(2 * e + 1) * BLK] = b_prev
                bias_g[p, :, (2 * e + 1) * BLK:(2 * e + 2) * BLK] = b_cur

    def block(p, start, with_prev):
        nk = 2 * BLK if with_prev else BLK
        keys = pl.ds(start - BLK, nk) if with_prev else pl.ds(start, nk)
        q = q_s[pl.ds(start, BLK), :]
        kcat = jnp.concatenate([kh_s[0, :, keys], kh_s[1, :, keys]], axis=1)
        b = bias_g[p] if with_prev else bias_f[p]
        s = jnp.dot(q, kcat, preferred_element_type=F32) + b
        m0 = jnp.max(s[:, :nk], axis=-1, keepdims=True)
        m1 = jnp.max(s[:, nk:], axis=-1, keepdims=True)
        pr = jnp.concatenate([jnp.exp2(s[:, :nk] - m0), jnp.exp2(s[:, nk:] - m1)],
                             axis=1).astype(BF16)
        vcat = jnp.concatenate(
            [jnp.concatenate([vh_s[0, keys, :], vh_s[1, keys, :]], axis=0),
             jnp.concatenate([ee_s[0, pl.ds(0, nk), :], ee_s[1, pl.ds(0, nk), :]], axis=0)],
            axis=1)
        o = jnp.dot(pr, vcat, preferred_element_type=F32)
        if p == N_PAT - 1:
            c = (start - p * SEQ) // BLK
            rows = pl.ds(p * SEQ + (c % 4) * SEG4 + c // 4, BLK, stride=4)
        else:
            rows = pl.ds(start, BLK)
        num_s[rows, :] = o[:, :2 * HEAD_DIM]
        den_s[rows, :] = o[:, 2 * HEAD_DIM:]
        mx_s[rows, :] = jnp.where(head0, m0, m1)

    general, first = _block_starts()
    for p, start in first:
        block(p, start, False)
    for p, start in general:
        block(p, start, True)

    for r in range(4):
        for a in range(SEG4 // BLK):
            nat = pl.ds(4 * BLK * a + r, BLK, stride=4)
            r4 = r * SEG4 + a * BLK
            p1, p2 = pl.ds(SEQ + r4, BLK), pl.ds(2 * SEQ + r4, BLK)
            x0, x1, x2 = mx_s[nat, :], mx_s[p1, :], mx_s[p2, :]
            m_all = jnp.maximum(jnp.maximum(x0, x1), x2)
            w0, w1, w2 = jnp.exp2(x0 - m_all), jnp.exp2(x1 - m_all), jnp.exp2(x2 - m_all)
            num = w0 * num_s[nat, :] + w1 * num_s[p1, :] + w2 * num_s[p2, :]
            den = w0 * den_s[nat, :] + w1 * den_s[p1, :] + w2 * den_s[p2, :]
            xf[nat, :] = num / den
    o_ref[...] = xf[...].astype(BF16)


def _attention(zq, slopes):
    t = zq.shape[2]
    nb = t // SEQ
    col = 2 * HEAD_DIM
    qkv_spec = lambda kind: pl.BlockSpec((None, HP_PER_STEP, SEQ, col),
                                         lambda b, h, kind=kind: (kind, h, b, 0))
    f32_seq = lambda: pltpu.VMEM((SEQ, col), F32)
    f32_all = lambda: pltpu.VMEM((N_PAT * SEQ, col), F32)
    return pl.pallas_call(
        _attn_kernel,
        grid=(nb, HEADS // 2 // HP_PER_STEP),
        in_specs=[
            pl.BlockSpec((HP_PER_STEP, 2, col), lambda b, h: (h, 0, 0)),
            qkv_spec(0), qkv_spec(1), qkv_spec(2),
        ],
        out_specs=pl.BlockSpec((HP_PER_STEP, SEQ, col), lambda b, h: (h, b, 0)),
        out_shape=jax.ShapeDtypeStruct((HEADS // 2, t, col), BF16),
        scratch_shapes=[
            f32_seq(), f32_seq(),
            pltpu.VMEM((N_PAT * SEQ, col), BF16),
            pltpu.VMEM((2, col, N_PAT * SEQ), BF16),
            pltpu.VMEM((2, N_PAT * SEQ, col), BF16),
            pltpu.VMEM((2, 2 * BLK, col), BF16),
            pltpu.VMEM((N_PAT - 1, BLK, 4 * BLK), F32),
            pltpu.VMEM((N_PAT, BLK, 2 * BLK), F32),
            f32_all(), f32_all(), f32_all(),
        ],
        compiler_params=pltpu.CompilerParams(
            dimension_semantics=("arbitrary", "arbitrary"),
            vmem_limit_bytes=VMEM_LIMIT_BYTES),
        name="dilated_attn",
    )(slopes, zq, zq, zq)


def _merge_kernel(u_ref, v_ref, ga_ref, gb_ref, yb_ref, x_ref, ws_ref, bs_ref,
                  wa_ref, wb_ref, wo_ref, g_ref, o_ref, ya_ref):
    tm = u_ref.shape[0]
    ti = lax.broadcasted_iota(jnp.int32, (CHUNK, CHUNK), 0)
    si = lax.broadcasted_iota(jnp.int32, (CHUNK, CHUNK), 1)
    causal = si <= ti
    for g in range(GROUPS):
        w = jnp.where(causal, ws_ref[g], 0.0).astype(BF16)
        bcol = bs_ref[:, g:g + 1]
        cols = slice(g * GROUP_DIM, (g + 1) * GROUP_DIM)
        for c in range(tm // CHUNK):
            rows = slice(c * CHUNK, (c + 1) * CHUNK)
            mixed = jnp.dot(w, v_ref[rows, cols], preferred_element_type=F32) + bcol
            ya_ref[rows, cols] = (u_ref[rows, cols].astype(F32) * mixed).astype(BF16)
    a = jnp.dot(ya_ref[...], wa_ref[...].astype(BF16), preferred_element_type=F32)
    yb = jnp.concatenate([yb_ref[hp] for hp in range(HEADS // 2)], axis=1)
    b = jnp.dot(yb, wb_ref[...].astype(BF16), preferred_element_type=F32)
    merged = ga_ref[...].astype(F32) * a + gb_ref[...].astype(F32) * b
    y = jnp.dot(merged.astype(BF16), wo_ref[...].astype(BF16), preferred_element_type=F32)
    o_ref[...] = x_ref[...] + _rms_norm(y, g_ref[...])


def _merge(z, yb, x2, w_s, b_s_t, w_a, w_b, w_o, g, tm):
    t = x2.shape[0]
    row = lambda c: pl.BlockSpec((tm, D_MODEL), lambda i, c=c: (i, c))
    const = lambda shape, **kw: pl.BlockSpec(shape, lambda i: (0,) * len(shape), **kw)
    weight = lambda: const((D_MODEL, D_MODEL), pipeline_mode=pl.Buffered(1))
    return pl.pallas_call(
        _merge_kernel,
        grid=(t // tm,),
        in_specs=[row(0), row(1), row(2), row(3),
                  pl.BlockSpec((HEADS // 2, tm, 2 * HEAD_DIM), lambda i: (0, i, 0)), row(0),
                  const((GROUPS, CHUNK, CHUNK)), const((CHUNK, GROUPS)),
                  weight(), weight(), weight(), const((1, D_MODEL))],
        out_specs=row(0),
        out_shape=jax.ShapeDtypeStruct((t, D_MODEL), F32),
        scratch_shapes=[pltpu.VMEM((tm, D_MODEL), BF16)],
        compiler_params=pltpu.CompilerParams(
            dimension_semantics=("arbitrary",),
            vmem_limit_bytes=VMEM_LIMIT_BYTES),
        name="merge",
    )(z, z, z, z, yb, x2, w_s, b_s_t, w_a, w_b, w_o, g)


def _ffn_kernel(x_ref, gpre_ref, w1_ref, w2_ref, gpost_ref, o_ref):
    x = x_ref[...]
    h = _rms_norm(x, gpre_ref[...]).astype(BF16)
    f = jnp.dot(h, w1_ref[...], preferred_element_type=F32)
    f = jnp.square(jnp.maximum(f, 0.0)).astype(BF16)
    y = jnp.dot(f, w2_ref[...], preferred_element_type=F32)
    o_ref[...] = x + _rms_norm(y, gpost_ref[...])


def _ffn(x1, g_pre, w1, w2, g_post, tm):
    t = x1.shape[0]
    const = lambda shape, **kw: pl.BlockSpec(shape, lambda i: (0,) * len(shape), **kw)
    return pl.pallas_call(
        _ffn_kernel,
        grid=(t // tm,),
        in_specs=[
            pl.BlockSpec((tm, D_MODEL), lambda i: (i, 0)),
            const((1, D_MODEL)),
            const((D_MODEL, D_FF), pipeline_mode=pl.Buffered(1)),
            const((D_FF, D_MODEL), pipeline_mode=pl.Buffered(1)),
            const((1, D_MODEL)),
        ],
        out_specs=pl.BlockSpec((tm, D_MODEL), lambda i: (i, 0)),
        out_shape=jax.ShapeDtypeStruct((t, D_MODEL), F32),
        compiler_params=pltpu.CompilerParams(
            dimension_semantics=("arbitrary",),
            vmem_limit_bytes=VMEM_LIMIT_BYTES),
        name="ffn",
    )(x1, g_pre, w1, w2, g_post)


def kernel(x, norm_mix_pre, w_in, b_gate, ln_v_g, ln_v_b, w_s, b_s, w_a_proj, w_b_proj,
           w_out, norm_mix_post, norm_ffn_pre, w_ff1, w_ff2, norm_ffn_post):
    bsz, s, d = x.shape
    assert (s, d) == (SEQ, D_MODEL)
    depth = w_in.shape[0]
    slopes = jnp.exp2(-8.0 * jnp.arange(1, HEADS + 1, dtype=F32) / HEADS)
    slopes = jnp.broadcast_to(slopes.reshape(HEADS // 2, 2, 1), (HEADS // 2, 2, 2 * HEAD_DIM))
    x2 = x.reshape(bsz * s, d)
    for l in range(depth):
        z, zq = _inproj(x2, norm_mix_pre[l][None], w_in[l].astype(BF16), ln_v_g[l][None],
                        ln_v_b[l][None], b_gate[l], tm=512)
        yb = _attention(zq, slopes)
        x2 = _merge(z, yb, x2, w_s[l], b_s[l].T, w_a_proj[l], w_b_proj[l], w_out[l],
                    norm_mix_post[l][None], tm=512)
        x2 = _ffn(x2, norm_ffn_pre[l][None], w_ff1[l].astype(BF16), w_ff2[l].astype(BF16),
                  norm_ffn_post[l][None], tm=512)
    return x2.reshape(bsz, s, d)
```

```python
import math

import jax
import jax.numpy as jnp
from jax import lax
from jax.experimental import pallas as pl
from jax.experimental.pallas import tpu as pltpu

D_MODEL = 1024
SEQ = 2048
CHUNK = 128
GROUPS = 8
GROUP_DIM = D_MODEL // GROUPS
HEADS = 16
HEAD_DIM = D_MODEL // HEADS
PATTERNS = ((128, 1), (512, 4), (2048, 16))
BLK = 128
D_FF = 4 * D_MODEL
EPS = 1e-6
N_IN_BLOCKS = 7
NEG = -1e30
LOG2E = math.log2(math.e)
Q_SCALE = LOG2E / math.sqrt(HEAD_DIM)

VMEM_LIMIT_BYTES = 56 * 1024 * 1024

F32 = jnp.float32
BF16 = jnp.bfloat16


def _rms_norm(x, g):
    return x * lax.rsqrt(jnp.mean(x * x, axis=-1, keepdims=True) + EPS) * g


def _sigmoid(x):
    return 0.5 * jnp.tanh(0.5 * x) + 0.5


def _inproj_kernel(x_ref, g_ref, w_ref, lng_ref, lnb_ref, bg_ref, z_ref, zq_ref):
    h = _rms_norm(x_ref[...], g_ref[...]).astype(BF16)

    def proj(j):
        return jnp.dot(h, w_ref[:, j * D_MODEL:(j + 1) * D_MODEL].astype(BF16),
                       preferred_element_type=F32)

    def put(c, val):
        z_ref[:, c * D_MODEL:(c + 1) * D_MODEL] = val.astype(BF16)

    def put_heads(kind, val):
        val = val.astype(BF16)
        for hp in range(HEADS // 2):
            zq_ref[kind, hp] = val[:, hp * 2 * HEAD_DIM:(hp + 1) * 2 * HEAD_DIM]

    put(0, jax.nn.gelu(proj(0)))
    v = jax.nn.gelu(proj(1))
    mu = jnp.mean(v, axis=-1, keepdims=True)
    var = jnp.mean(jnp.square(v - mu), axis=-1, keepdims=True)
    put(1, (v - mu) * lax.rsqrt(var + EPS) * lng_ref[...] + lnb_ref[...])
    put(2, _sigmoid(proj(5) + bg_ref[0:1, :]))
    put(3, _sigmoid(proj(6) + bg_ref[1:2, :]))
    put_heads(0, proj(2) * Q_SCALE)
    put_heads(1, proj(3))
    put_heads(2, proj(4))


def _inproj(x2, g, w_in, ln_g, ln_b, b_gate, tm):
    t = x2.shape[0]
    const = lambda shape, **kw: pl.BlockSpec(shape, lambda i: (0,) * len(shape), **kw)
    return pl.pallas_call(
        _inproj_kernel,
        grid=(t // tm,),
        in_specs=[
            pl.BlockSpec((tm, D_MODEL), lambda i: (i, 0)),
            const((1, D_MODEL)),
            const((D_MODEL, N_IN_BLOCKS * D_MODEL), pipeline_mode=pl.Buffered(1)),
            const((1, D_MODEL)),
            const((1, D_MODEL)),
            const((2, D_MODEL)),
        ],
        out_specs=[
            pl.BlockSpec((tm, 4 * D_MODEL), lambda i: (i, 0)),
            pl.BlockSpec((3, HEADS // 2, tm, 2 * HEAD_DIM), lambda i: (0, 0, i, 0)),
        ],
        out_shape=[
            jax.ShapeDtypeStruct((t, 4 * D_MODEL), BF16),
            jax.ShapeDtypeStruct((3, HEADS // 2, t, 2 * HEAD_DIM), BF16),
        ],
        compiler_params=pltpu.CompilerParams(
            dimension_semantics=("arbitrary",),
            vmem_limit_bytes=VMEM_LIMIT_BYTES),
        name="inproj",
    )(x2, g, w_in, ln_g, ln_b, b_gate)


N_PAT = len(PATTERNS)
SEG4 = SEQ // 4
HP_PER_STEP = 2


def _block_starts():
    general, first = [], []
    for p, (_, dil) in enumerate(PATTERNS):
        seg = SEQ // dil
        for r in range(dil):
            for n in range(seg // BLK):
                (first if n == 0 else general).append((p, p * SEQ + r * seg + n * BLK))
    return general, first


def _attn_kernel(slope_ref, q_ref, k_ref, v_ref, o_ref, *scratch):
    for i in range(HP_PER_STEP):
        _attn_head_pair(slope_ref.at[i], q_ref.at[i], k_ref.at[i], v_ref.at[i], o_ref.at[i],
                        *scratch)


def _attn_head_pair(slope_ref, q_ref, k_ref, v_ref, o_ref,
                    xf, l4, q_s, kh_s, vh_s, ee_s, bias_g, bias_f, num_s, den_s, mx_s):
    lane = lax.broadcasted_iota(jnp.int32, (BLK, 2 * HEAD_DIM), 1)
    head0 = lane < HEAD_DIM

    top_half = lax.broadcasted_iota(jnp.int32, (2 * HEAD_DIM, BLK), 0) < HEAD_DIM

    def emit(dst_rows, val, kind):
        if kind == 0:
            q_s[dst_rows, :] = val.astype(BF16)
        elif kind == 1:
            kt = val.T
            kh_s[0, :, dst_rows] = jnp.where(top_half, kt, 0.0).astype(BF16)
            kh_s[1, :, dst_rows] = jnp.where(top_half, 0.0, kt).astype(BF16)
        else:
            vh_s[0, dst_rows, :] = jnp.where(head0, val, 0.0).astype(BF16)
            vh_s[1, dst_rows, :] = jnp.where(head0, 0.0, val).astype(BF16)

    for kind, src in enumerate((q_ref, k_ref, v_ref)):
        for c in range(SEQ // BLK):
            rows = pl.ds(c * BLK, BLK)
            x = src[rows, :].astype(F32)
            xf[rows, :] = x
            emit(rows, x, kind)
        for r in range(4):
            for a in range(SEG4 // BLK):
                x = xf[pl.ds(4 * BLK * a + r, BLK, stride=4), :]
                dst = r * SEG4 + a * BLK
                l4[pl.ds(dst, BLK), :] = x
                emit(pl.ds(SEQ + dst, BLK), x, kind)
        for c in range(16):
            x = l4[pl.ds((c % 4) * SEG4 + c // 4, BLK, stride=4), :]
            emit(pl.ds(2 * SEQ + c * BLK, BLK), x, kind)

    e0 = jnp.where(head0, 1.0, 0.0).astype(BF16)
    e1 = jnp.where(head0, 0.0, 1.0).astype(BF16)
    for i in range(2):
        ee_s[0, pl.ds(i * BLK, BLK), :] = e0
        ee_s[1, pl.ds(i * BLK, BLK), :] = e1

    qi = lax.broadcasted_iota(jnp.int32, (BLK, BLK), 0)
    kj = lax.broadcasted_iota(jnp.int32, (BLK, BLK), 1)
    d_cur = (qi - kj).astype(F32)
    ok_cur = kj <= qi
    ok_prev = kj >= qi
    for p, (_, dil) in enumerate(PATTERNS):
        for e in range(2):
            sl = slope_ref[e:e + 1, :]
            sl = sl * LOG2E
            b_cur = jnp.where(ok_cur, -sl * (dil * d_cur), NEG)
            bias_f[p, :, e * BLK:(e + 1) * BLK] = b_cur
            if p < N_PAT - 1:
                b_prev = jnp.where(ok_prev, -sl * (dil * (d_cur + BLK)), NEG)
                bias_g[p, :, 2 * e * BLK:(2 * e + 1) * BLK] = b_prev
                bias_g[p, :, (2 * e + 1) * BLK:(2 * e + 2) * BLK] = b_cur

    def block(p, start, with_prev):
        nk = 2 * BLK if with_prev else BLK
        keys = pl.ds(start - BLK, nk) if with_prev else pl.ds(start, nk)
        q = q_s[pl.ds(start, BLK), :]
        kcat = jnp.concatenate([kh_s[0, :, keys], kh_s[1, :, keys]], axis=1)
        b = bias_g[p] if with_prev else bias_f[p]
        s = jnp.dot(q, kcat, preferred_element_type=F32) + b
        m0 = jnp.max(s[:, :nk], axis=-1, keepdims=True)
        m1 = jnp.max(s[:, nk:], axis=-1, keepdims=True)
        pr = jnp.concatenate([jnp.exp2(s[:, :nk] - m0), jnp.exp2(s[:, nk:] - m1)],
                             axis=1).astype(BF16)
        vcat = jnp.concatenate(
            [jnp.concatenate([vh_s[0, keys, :], vh_s[1, keys, :]], axis=0),
             jnp.concatenate([ee_s[0, pl.ds(0, nk), :], ee_s[1, pl.ds(0, nk), :]], axis=0)],
            axis=1)
        o = jnp.dot(pr, vcat, preferred_element_type=F32)
        if p == N_PAT - 1:
            c = (start - p * SEQ) // BLK
            rows = pl.ds(p * SEQ + (c % 4) * SEG4 + c // 4, BLK, stride=4)
        else:
            rows = pl.ds(start, BLK)
        num_s[rows, :] = o[:, :2 * HEAD_DIM]
        den_s[rows, :] = o[:, 2 * HEAD_DIM:]
        mx_s[rows, :] = jnp.where(head0, m0, m1)

    general, first = _block_starts()
    for p, start in first:
        block(p, start, False)
    for p, start in general:
        block(p, start, True)

    for r in range(4):
        for a in range(SEG4 // BLK):
            nat = pl.ds(4 * BLK * a + r, BLK, stride=4)
            r4 = r * SEG4 + a * BLK
            p1, p2 = pl.ds(SEQ + r4, BLK), pl.ds(2 * SEQ + r4, BLK)
            x0, x1, x2 = mx_s[nat, :], mx_s[p1, :], mx_s[p2, :]
            m_all = jnp.maximum(jnp.maximum(x0, x1), x2)
            w0, w1, w2 = jnp.exp2(x0 - m_all), jnp.exp2(x1 - m_all), jnp.exp2(x2 - m_all)
            num = w0 * num_s[nat, :] + w1 * num_s[p1, :] + w2 * num_s[p2, :]
            den = w0 * den_s[nat, :] + w1 * den_s[p1, :] + w2 * den_s[p2, :]
            xf[nat, :] = num / den
    o_ref[...] = xf[...].astype(BF16)


def _attention(zq, slopes):
    t = zq.shape[2]
    nb = t // SEQ
    col = 2 * HEAD_DIM
    qkv_spec = lambda kind: pl.BlockSpec((None, HP_PER_STEP, SEQ, col),
                                         lambda b, h, kind=kind: (kind, h, b, 0))
    f32_seq = lambda: pltpu.VMEM((SEQ, col), F32)
    f32_all = lambda: pltpu.VMEM((N_PAT * SEQ, col), F32)
    return pl.pallas_call(
        _attn_kernel,
        grid=(nb, HEADS // 2 // HP_PER_STEP),
        in_specs=[
            pl.BlockSpec((HP_PER_STEP, 2, col), lambda b, h: (h, 0, 0)),
            qkv_spec(0), qkv_spec(1), qkv_spec(2),
        ],
        out_specs=pl.BlockSpec((HP_PER_STEP, SEQ, col), lambda b, h: (h, b, 0)),
        out_shape=jax.ShapeDtypeStruct((HEADS // 2, t, col), BF16),
        scratch_shapes=[
            f32_seq(), f32_seq(),
            pltpu.VMEM((N_PAT * SEQ, col), BF16),
            pltpu.VMEM((2, col, N_PAT * SEQ), BF16),
            pltpu.VMEM((2, N_PAT * SEQ, col), BF16),
            pltpu.VMEM((2, 2 * BLK, col), BF16),
            pltpu.VMEM((N_PAT - 1, BLK, 4 * BLK), F32),
            pltpu.VMEM((N_PAT, BLK, 2 * BLK), F32),
            f32_all(), f32_all(), f32_all(),
        ],
        compiler_params=pltpu.CompilerParams(
            dimension_semantics=("arbitrary", "arbitrary"),
            vmem_limit_bytes=VMEM_LIMIT_BYTES),
        name="dilated_attn",
    )(slopes, zq, zq, zq)


def _merge_kernel(u_ref, v_ref, ga_ref, gb_ref, yb_ref, x_ref, ws_ref, bs_ref,
                  wa_ref, wb_ref, wo_ref, g_ref, o_ref, ya_ref):
    tm = u_ref.shape[0]
    ti = lax.broadcasted_iota(jnp.int32, (CHUNK, CHUNK), 0)
    si = lax.broadcasted_iota(jnp.int32, (CHUNK, CHUNK), 1)
    causal = si <= ti
    for g in range(GROUPS):
        w = jnp.where(causal, ws_ref[g], 0.0).astype(BF16)
        bcol = bs_ref[:, g:g + 1]
        cols = slice(g * GROUP_DIM, (g + 1) * GROUP_DIM)
        for c in range(tm // CHUNK):
            rows = slice(c * CHUNK, (c + 1) * CHUNK)
            mixed = jnp.dot(w, v_ref[rows, cols], preferred_element_type=F32) + bcol
            ya_ref[rows, cols] = (u_ref[rows, cols].astype(F32) * mixed).astype(BF16)
    a = jnp.dot(ya_ref[...], wa_ref[...].astype(BF16), preferred_element_type=F32)
    yb = jnp.concatenate([yb_ref[hp] for hp in range(HEADS // 2)], axis=1)
    b = jnp.dot(yb, wb_ref[...].astype(BF16), preferred_element_type=F32)
    merged = ga_ref[...].astype(F32) * a + gb_ref[...].astype(F32) * b
    y = jnp.dot(merged.astype(BF16), wo_ref[...].astype(BF16), preferred_element_type=F32)
    o_ref[...] = x_ref[...] + _rms_norm(y, g_ref[...])


def _merge(z, yb, x2, w_s, b_s_t, w_a, w_b, w_o, g, tm):
    t = x2.shape[0]
    row = lambda c: pl.BlockSpec((tm, D_MODEL), lambda i, c=c: (i, c))
    const = lambda shape, **kw: pl.BlockSpec(shape, lambda i: (0,) * len(shape), **kw)
    weight = lambda: const((D_MODEL, D_MODEL), pipeline_mode=pl.Buffered(1))
    return pl.pallas_call(
        _merge_kernel,
        grid=(t // tm,),
        in_specs=[row(0), row(1), row(2), row(3),
                  pl.BlockSpec((HEADS // 2, tm, 2 * HEAD_DIM), lambda i: (0, i, 0)), row(0),
                  const((GROUPS, CHUNK, CHUNK)), const((CHUNK, GROUPS)),
                  weight(), weight(), weight(), const((1, D_MODEL))],
        out_specs=row(0),
        out_shape=jax.ShapeDtypeStruct((t, D_MODEL), F32),
        scratch_shapes=[pltpu.VMEM((tm, D_MODEL), BF16)],
        compiler_params=pltpu.CompilerParams(
            dimension_semantics=("arbitrary",),
            vmem_limit_bytes=VMEM_LIMIT_BYTES),
        name="merge",
    )(z, z, z, z, yb, x2, w_s, b_s_t, w_a, w_b, w_o, g)


def _ffn_kernel(x_ref, gpre_ref, w1_ref, w2_ref, gpost_ref, o_ref):
    x = x_ref[...]
    h = _rms_norm(x, gpre_ref[...]).astype(BF16)
    f = jnp.dot(h, w1_ref[...], preferred_element_type=F32)
    f = jnp.square(jnp.maximum(f, 0.0)).astype(BF16)
    y = jnp.dot(f, w2_ref[...], preferred_element_type=F32)
    o_ref[...] = x + _rms_norm(y, gpost_ref[...])


def _ffn(x1, g_pre, w1, w2, g_post, tm):
    t = x1.shape[0]
    const = lambda shape, **kw: pl.BlockSpec(shape, lambda i: (0,) * len(shape), **kw)
    return pl.pallas_call(
        _ffn_kernel,
        grid=(t // tm,),
        in_specs=[
            pl.BlockSpec((tm, D_MODEL), lambda i: (i, 0)),
            const((1, D_MODEL)),
            const((D_MODEL, D_FF), pipeline_mode=pl.Buffered(1)),
            const((D_FF, D_MODEL), pipeline_mode=pl.Buffered(1)),
            const((1, D_MODEL)),
        ],
        out_specs=pl.BlockSpec((tm, D_MODEL), lambda i: (i, 0)),
        out_shape=jax.ShapeDtypeStruct((t, D_MODEL), F32),
        compiler_params=pltpu.CompilerParams(
            dimension_semantics=("arbitrary",),
            vmem_limit_bytes=VMEM_LIMIT_BYTES),
        name="ffn",
    )(x1, g_pre, w1, w2, g_post)


def kernel(x, norm_mix_pre, w_in, b_gate, ln_v_g, ln_v_b, w_s, b_s, w_a_proj, w_b_proj,
           w_out, norm_mix_post, norm_ffn_pre, w_ff1, w_ff2, norm_ffn_post):
    bsz, s, d = x.shape
    assert (s, d) == (SEQ, D_MODEL)
    depth = w_in.shape[0]
    slopes = jnp.exp2(-8.0 * jnp.arange(1, HEADS + 1, dtype=F32) / HEADS)
    slopes = jnp.broadcast_to(slopes.reshape(HEADS // 2, 2, 1), (HEADS // 2, 2, 2 * HEAD_DIM))
    x2 = x.reshape(bsz * s, d)
    for l in range(depth):
        z, zq = _inproj(x2, norm_mix_pre[l][None], w_in[l], ln_v_g[l][None],
                        ln_v_b[l][None], b_gate[l], tm=512)
        yb = _attention(zq, slopes)
        x2 = _merge(z, yb, x2, w_s[l], b_s[l].T, w_a_proj[l], w_b_proj[l], w_out[l],
                    norm_mix_post[l][None], tm=512)
        x2 = _ffn(x2, norm_ffn_pre[l][None], w_ff1[l].astype(BF16), w_ff2[l].astype(BF16),
                  norm_ffn_post[l][None], tm=512)
    return x2.reshape(bsz, s, d)
```

```python
import math

import jax
import jax.numpy as jnp
from jax import lax
from jax.experimental import pallas as pl
from jax.experimental.pallas import tpu as pltpu

D_MODEL = 1024
SEQ = 2048
CHUNK = 128
GROUPS = 8
GROUP_DIM = D_MODEL // GROUPS
HEADS = 16
HEAD_DIM = D_MODEL // HEADS
PATTERNS = ((128, 1), (512, 4), (2048, 16))
BLK = 128
D_FF = 4 * D_MODEL
EPS = 1e-6
N_IN_BLOCKS = 7
NEG = -1e30
LOG2E = math.log2(math.e)
Q_SCALE = LOG2E / math.sqrt(HEAD_DIM)

VMEM_LIMIT_BYTES = 56 * 1024 * 1024

F32 = jnp.float32
BF16 = jnp.bfloat16


def _rms_norm(x, g):
    return x * lax.rsqrt(jnp.mean(x * x, axis=-1, keepdims=True) + EPS) * g


def _sigmoid(x):
    return 0.5 * jnp.tanh(0.5 * x) + 0.5


def _inproj_kernel(x_ref, g_ref, w_ref, lng_ref, lnb_ref, bg_ref, z_ref, zq_ref):
    h = _rms_norm(x_ref[...], g_ref[...]).astype(BF16)

    def proj(j):
        return jnp.dot(h, w_ref[:, j * D_MODEL:(j + 1) * D_MODEL].astype(BF16),
                       preferred_element_type=F32)

    def put(c, val):
        z_ref[:, c * D_MODEL:(c + 1) * D_MODEL] = val.astype(BF16)

    def put_heads(kind, val):
        val = val.astype(BF16)
        for hp in range(HEADS // 2):
            zq_ref[kind, hp] = val[:, hp * 2 * HEAD_DIM:(hp + 1) * 2 * HEAD_DIM]

    put(0, jax.nn.gelu(proj(0)))
    v = jax.nn.gelu(proj(1))
    mu = jnp.mean(v, axis=-1, keepdims=True)
    var = jnp.mean(jnp.square(v - mu), axis=-1, keepdims=True)
    put(1, (v - mu) * lax.rsqrt(var + EPS) * lng_ref[...] + lnb_ref[...])
    put(2, _sigmoid(proj(5) + bg_ref[0:1, :]))
    put(3, _sigmoid(proj(6) + bg_ref[1:2, :]))
    put_heads(0, proj(2) * Q_SCALE)
    put_heads(1, proj(3))
    put_heads(2, proj(4))


def _inproj(x2, g, w_in, ln_g, ln_b, b_gate, tm):
    t = x2.shape[0]
    const = lambda shape, **kw: pl.BlockSpec(shape, lambda i: (0,) * len(shape), **kw)
    return pl.pallas_call(
        _inproj_kernel,
        grid=(t // tm,),
        in_specs=[
            pl.BlockSpec((tm, D_MODEL), lambda i: (i, 0)),
            const((1, D_MODEL)),
            const((D_MODEL, N_IN_BLOCKS * D_MODEL), pipeline_mode=pl.Buffered(1)),
            const((1, D_MODEL)),
            const((1, D_MODEL)),
            const((2, D_MODEL)),
        ],
        out_specs=[
            pl.BlockSpec((tm, 4 * D_MODEL), lambda i: (i, 0)),
            pl.BlockSpec((3, HEADS // 2, tm, 2 * HEAD_DIM), lambda i: (0, 0, i, 0)),
        ],
        out_shape=[
            jax.ShapeDtypeStruct((t, 4 * D_MODEL), BF16),
            jax.ShapeDtypeStruct((3, HEADS // 2, t, 2 * HEAD_DIM), BF16),
        ],
        compiler_params=pltpu.CompilerParams(
            dimension_semantics=("arbitrary",),
            vmem_limit_bytes=VMEM_LIMIT_BYTES),
        name="inproj",
    )(x2, g, w_in, ln_g, ln_b, b_gate)


N_PAT = len(PATTERNS)
SEG4 = SEQ // 4
HP_PER_STEP = 2


def _block_starts():
    general, first = [], []
    for p, (_, dil) in enumerate(PATTERNS):
        seg = SEQ // dil
        for r in range(dil):
            for n in range(seg // BLK):
                (first if n == 0 else general).append((p, p * SEQ + r * seg + n * BLK))
    return general, first


def _attn_kernel(slope_ref, q_ref, k_ref, v_ref, o_ref, *scratch):
    for i in range(HP_PER_STEP):
        _attn_head_pair(slope_ref.at[i], q_ref.at[i], k_ref.at[i], v_ref.at[i], o_ref.at[i],
                        *scratch)


def _attn_head_pair(slope_ref, q_ref, k_ref, v_ref, o_ref,
                    xf, l4, q_s, kh_s, vh_s, ee_s, bias_g, bias_f, num_s, den_s, mx_s):
    lane = lax.broadcasted_iota(jnp.int32, (BLK, 2 * HEAD_DIM), 1)
    head0 = lane < HEAD_DIM

    top_half = lax.broadcasted_iota(jnp.int32, (2 * HEAD_DIM, BLK), 0) < HEAD_DIM

    def emit(dst_rows, val, kind):
        if kind == 0:
            q_s[dst_rows, :] = val.astype(BF16)
        elif kind == 1:
            kt = val.T
            kh_s[0, :, dst_rows] = jnp.where(top_half, kt, 0.0).astype(BF16)
            kh_s[1, :, dst_rows] = jnp.where(top_half, 0.0, kt).astype(BF16)
        else:
            vh_s[0, dst_rows, :] = jnp.where(head0, val, 0.0).astype(BF16)
            vh_s[1, dst_rows, :] = jnp.where(head0, 0.0, val).astype(BF16)

    for kind, src in enumerate((q_ref, k_ref, v_ref)):
        for c in range(SEQ // BLK):
            rows = pl.ds(c * BLK, BLK)
            x = src[rows, :].astype(F32)
            xf[rows, :] = x
            emit(rows, x, kind)
        for r in range(4):
            for a in range(SEG4 // BLK):
                x = xf[pl.ds(4 * BLK * a + r, BLK, stride=4), :]
                dst = r * SEG4 + a * BLK
                l4[pl.ds(dst, BLK), :] = x
                emit(pl.ds(SEQ + dst, BLK), x, kind)
        for c in range(16):
            x = l4[pl.ds((c % 4) * SEG4 + c // 4, BLK, stride=4), :]
            emit(pl.ds(2 * SEQ + c * BLK, BLK), x, kind)

    e0 = jnp.where(head0, 1.0, 0.0).astype(BF16)
    e1 = jnp.where(head0, 0.0, 1.0).astype(BF16)
    for i in range(2):
        ee_s[0, pl.ds(i * BLK, BLK), :] = e0
        ee_s[1, pl.ds(i * BLK, BLK), :] = e1

    qi = lax.broadcasted_iota(jnp.int32, (BLK, BLK), 0)
    kj = lax.broadcasted_iota(jnp.int32, (BLK, BLK), 1)
    d_cur = (qi - kj).astype(F32)
    ok_cur = kj <= qi
    ok_prev = kj >= qi
    for p, (_, dil) in enumerate(PATTERNS):
        for e in range(2):
            sl = slope_ref[e:e + 1, :]
            sl = sl * LOG2E
            b_cur = jnp.where(ok_cur, -sl * (dil * d_cur), NEG)
            bias_f[p, :, e * BLK:(e + 1) * BLK] = b_cur
            if p < N_PAT - 1:
                b_prev = jnp.where(ok_prev, -sl * (dil * (d_cur + BLK)), NEG)
                bias_g[p, :, 2 * e * BLK:(2 * e + 1) * BLK] = b_prev
                bias_g[p, :, (2 * e + 1) * BLK:(2 * e + 2) * BLK] = b_cur

    def block(p, start, with_prev):
        nk = 2 * BLK if with_prev else BLK
        keys = pl.ds(start - BLK, nk) if with_prev else pl.ds(start, nk)
        q = q_s[pl.ds(start, BLK), :]
        kcat = jnp.concatenate([kh_s[0, :, keys], kh_s[1, :, keys]], axis=1)
        b = bias_g[p] if with_prev else bias_f[p]
        s = jnp.dot(q, kcat, preferred_element_type=F32) + b
        m0 = jnp.max(s[:, :nk], axis=-1, keepdims=True)
        m1 = jnp.max(s[:, nk:], axis=-1, keepdims=True)
        pr = jnp.concatenate([jnp.exp2(s[:, :nk] - m0), jnp.exp2(s[:, nk:] - m1)],
                             axis=1).astype(BF16)
        vcat = jnp.concatenate(
            [jnp.concatenate([vh_s[0, keys, :], vh_s[1, keys, :]], axis=0),
             jnp.concatenate([ee_s[0, pl.ds(0, nk), :], ee_s[1, pl.ds(0, nk), :]], axis=0)],
            axis=1)
        o = jnp.dot(pr, vcat, preferred_element_type=F32)
        if p == N_PAT - 1:
            c = (start - p * SEQ) // BLK
            rows = pl.ds(p * SEQ + (c % 4) * SEG4 + c // 4, BLK, stride=4)
        else:
            rows = pl.ds(start, BLK)
        num_s[rows, :] = o[:, :2 * HEAD_DIM]
        den_s[rows, :] = o[:, 2 * HEAD_DIM:]
        mx_s[rows, :] = jnp.where(head0, m0, m1)

    general, first = _block_starts()
    for p, start in first:
        block(p, start, False)
    for p, start in general:
        block(p, start, True)

    for r in range(4):
        for a in range(SEG4 // BLK):
            nat = pl.ds(4 * BLK * a + r, BLK, stride=4)
            r4 = r * SEG4 + a * BLK
            p1, p2 = pl.ds(SEQ + r4, BLK), pl.ds(2 * SEQ + r4, BLK)
            x0, x1, x2 = mx_s[nat, :], mx_s[p1, :], mx_s[p2, :]
            m_all = jnp.maximum(jnp.maximum(x0, x1), x2)
            w0, w1, w2 = jnp.exp2(x0 - m_all), jnp.exp2(x1 - m_all), jnp.exp2(x2 - m_all)
            num = w0 * num_s[nat, :] + w1 * num_s[p1, :] + w2 * num_s[p2, :]
            den = w0 * den_s[nat, :] + w1 * den_s[p1, :] + w2 * den_s[p2, :]
            xf[nat, :] = num / den
    o_ref[...] = xf[...].astype(BF16)


def _attention(zq, slopes):
    t = zq.shape[2]
    nb = t // SEQ
    col = 2 * HEAD_DIM
    qkv_spec = lambda kind: pl.BlockSpec((None, HP_PER_STEP, SEQ, col),
                                         lambda b, h, kind=kind: (kind, h, b, 0))
    f32_seq = lambda: pltpu.VMEM((SEQ, col), F32)
    f32_all = lambda: pltpu.VMEM((N_PAT * SEQ, col), F32)
    return pl.pallas_call(
        _attn_kernel,
        grid=(nb, HEADS // 2 // HP_PER_STEP),
        in_specs=[
            pl.BlockSpec((HP_PER_STEP, 2, col), lambda b, h: (h, 0, 0)),
            qkv_spec(0), qkv_spec(1), qkv_spec(2),
        ],
        out_specs=pl.BlockSpec((HP_PER_STEP, SEQ, col), lambda b, h: (h, b, 0)),
        out_shape=jax.ShapeDtypeStruct((HEADS // 2, t, col), BF16),
        scratch_shapes=[
            f32_seq(), f32_seq(),
            pltpu.VMEM((N_PAT * SEQ, col), BF16),
            pltpu.VMEM((2, col, N_PAT * SEQ), BF16),
            pltpu.VMEM((2, N_PAT * SEQ, col), BF16),
            pltpu.VMEM((2, 2 * BLK, col), BF16),
            pltpu.VMEM((N_PAT - 1, BLK, 4 * BLK), F32),
            pltpu.VMEM((N_PAT, BLK, 2 * BLK), F32),
            f32_all(), f32_all(), f32_all(),
        ],
        compiler_params=pltpu.CompilerParams(
            dimension_semantics=("arbitrary", "arbitrary"),
            vmem_limit_bytes=VMEM_LIMIT_BYTES),
        name="dilated_attn",
    )(slopes, zq, zq, zq)


def _merge_kernel(u_ref, v_ref, ga_ref, gb_ref, yb_ref, x_ref, ws_ref, bs_ref,
                  wa_ref, wb_ref, wo_ref, g_ref, o_ref, ya_ref):
    tm = u_ref.shape[0]
    ti = lax.broadcasted_iota(jnp.int32, (CHUNK, CHUNK), 0)
    si = lax.broadcasted_iota(jnp.int32, (CHUNK, CHUNK), 1)
    causal = si <= ti
    for g in range(GROUPS):
        w = jnp.where(causal, ws_ref[g], 0.0).astype(BF16)
        bcol = bs_ref[:, g:g + 1]
        cols = slice(g * GROUP_DIM, (g + 1) * GROUP_DIM)
        for c in range(tm // CHUNK):
            rows = slice(c * CHUNK, (c + 1) * CHUNK)
            mixed = jnp.dot(w, v_ref[rows, cols], preferred_element_type=F32) + bcol
            ya_ref[rows, cols] = (u_ref[rows, cols].astype(F32) * mixed).astype(BF16)
    a = jnp.dot(ya_ref[...], wa_ref[...].astype(BF16), preferred_element_type=F32)
    yb = jnp.concatenate([yb_ref[hp] for hp in range(HEADS // 2)], axis=1)
    b = jnp.dot(yb, wb_ref[...].astype(BF16), preferred_element_type=F32)
    merged = ga_ref[...].astype(F32) * a + gb_ref[...].astype(F32) * b
    y = jnp.dot(merged.astype(BF16), wo_ref[...].astype(BF16), preferred_element_type=F32)
    o_ref[...] = x_ref[...] + _rms_norm(y, g_ref[...])


def _merge(z, yb, x2, w_s, b_s_t, w_a, w_b, w_o, g, tm):
    t = x2.shape[0]
    row = lambda c: pl.BlockSpec((tm, D_MODEL), lambda i, c=c: (i, c))
    const = lambda shape, **kw: pl.BlockSpec(shape, lambda i: (0,) * len(shape), **kw)
    weight = lambda: const((D_MODEL, D_MODEL), pipeline_mode=pl.Buffered(1))
    return pl.pallas_call(
        _merge_kernel,
        grid=(t // tm,),
        in_specs=[row(0), row(1), row(2), row(3),
                  pl.BlockSpec((HEADS // 2, tm, 2 * HEAD_DIM), lambda i: (0, i, 0)), row(0),
                  const((GROUPS, CHUNK, CHUNK)), const((CHUNK, GROUPS)),
                  weight(), weight(), weight(), const((1, D_MODEL))],
        out_specs=row(0),
        out_shape=jax.ShapeDtypeStruct((t, D_MODEL), F32),
        scratch_shapes=[pltpu.VMEM((tm, D_MODEL), BF16)],
        compiler_params=pltpu.CompilerParams(
            dimension_semantics=("arbitrary",),
            vmem_limit_bytes=VMEM_LIMIT_BYTES),
        name="merge",
    )(z, z, z, z, yb, x2, w_s, b_s_t, w_a, w_b, w_o, g)


def _ffn_kernel(x_ref, gpre_ref, w1_ref, w2_ref, gpost_ref, o_ref):
    x = x_ref[...]
    h = _rms_norm(x, gpre_ref[...]).astype(BF16)
    y = None
    for j in range(D_FF // D_MODEL):
        slab = slice(j * D_MODEL, (j + 1) * D_MODEL)
        f = jnp.dot(h, w1_ref[:, slab].astype(BF16), preferred_element_type=F32)
        f = jnp.square(jnp.maximum(f, 0.0)).astype(BF16)
        part = jnp.dot(f, w2_ref[slab, :].astype(BF16), preferred_element_type=F32)
        y = part if y is None else y + part
    o_ref[...] = x + _rms_norm(y, gpost_ref[...])


def _ffn(x1, g_pre, w1, w2, g_post, tm):
    t = x1.shape[0]
    const = lambda shape, **kw: pl.BlockSpec(shape, lambda i: (0,) * len(shape), **kw)
    return pl.pallas_call(
        _ffn_kernel,
        grid=(t // tm,),
        in_specs=[
            pl.BlockSpec((tm, D_MODEL), lambda i: (i, 0)),
            const((1, D_MODEL)),
            const((D_MODEL, D_FF), pipeline_mode=pl.Buffered(1)),
            const((D_FF, D_MODEL), pipeline_mode=pl.Buffered(1)),
            const((1, D_MODEL)),
        ],
        out_specs=pl.BlockSpec((tm, D_MODEL), lambda i: (i, 0)),
        out_shape=jax.ShapeDtypeStruct((t, D_MODEL), F32),
        compiler_params=pltpu.CompilerParams(
            dimension_semantics=("arbitrary",),
            vmem_limit_bytes=VMEM_LIMIT_BYTES),
        name="ffn",
    )(x1, g_pre, w1, w2, g_post)


def kernel(x, norm_mix_pre, w_in, b_gate, ln_v_g, ln_v_b, w_s, b_s, w_a_proj, w_b_proj,
           w_out, norm_mix_post, norm_ffn_pre, w_ff1, w_ff2, norm_ffn_post):
    bsz, s, d = x.shape
    assert (s, d) == (SEQ, D_MODEL)
    depth = w_in.shape[0]
    slopes = jnp.exp2(-8.0 * jnp.arange(1, HEADS + 1, dtype=F32) / HEADS)
    slopes = jnp.broadcast_to(slopes.reshape(HEADS // 2, 2, 1), (HEADS // 2, 2, 2 * HEAD_DIM))
    x2 = x.reshape(bsz * s, d)
    for l in range(depth):
        z, zq = _inproj(x2, norm_mix_pre[l][None], w_in[l], ln_v_g[l][None],
                        ln_v_b[l][None], b_gate[l], tm=512)
        yb = _attention(zq, slopes)
        x2 = _merge(z, yb, x2, w_s[l], b_s[l].T, w_a_proj[l], w_b_proj[l], w_out[l],
                    norm_mix_post[l][None], tm=512)
        x2 = _ffn(x2, norm_ffn_pre[l][None], w_ff1[l], w_ff2[l], norm_ffn_post[l][None],
                  tm=512)
    return x2.reshape(bsz, s, d)
```

```python
import math

import jax
import jax.numpy as jnp
from jax import lax
from jax.experimental import pallas as pl
from jax.experimental.pallas import tpu as pltpu

D_MODEL = 1024
SEQ = 2048
CHUNK = 128
GROUPS = 8
GROUP_DIM = D_MODEL // GROUPS
HEADS = 16
HEAD_DIM = D_MODEL // HEADS
PATTERNS = ((128, 1), (512, 4), (2048, 16))
BLK = 128
D_FF = 4 * D_MODEL
EPS = 1e-6
N_IN_BLOCKS = 7
NEG = -1e30
LOG2E = math.log2(math.e)
Q_SCALE = LOG2E / math.sqrt(HEAD_DIM)

VMEM_LIMIT_BYTES = 56 * 1024 * 1024

F32 = jnp.float32
BF16 = jnp.bfloat16


def _rms_norm(x, g):
    return x * lax.rsqrt(jnp.mean(x * x, axis=-1, keepdims=True) + EPS) * g


def _sigmoid(x):
    return 0.5 * jnp.tanh(0.5 * x) + 0.5


def _inproj_kernel(x_ref, g_ref, w_ref, lng_ref, lnb_ref, bg_ref, z_ref, zq_ref):
    h = _rms_norm(x_ref[...], g_ref[...]).astype(BF16)

    def proj(j):
        return jnp.dot(h, w_ref[:, j * D_MODEL:(j + 1) * D_MODEL].astype(BF16),
                       preferred_element_type=F32)

    def put(c, val):
        z_ref[:, c * D_MODEL:(c + 1) * D_MODEL] = val.astype(BF16)

    def put_heads(kind, val):
        val = val.astype(BF16)
        for hp in range(HEADS // 2):
            zq_ref[kind, hp] = val[:, hp * 2 * HEAD_DIM:(hp + 1) * 2 * HEAD_DIM]

    put(0, jax.nn.gelu(proj(0)))
    v = jax.nn.gelu(proj(1))
    mu = jnp.mean(v, axis=-1, keepdims=True)
    var = jnp.mean(jnp.square(v - mu), axis=-1, keepdims=True)
    put(1, (v - mu) * lax.rsqrt(var + EPS) * lng_ref[...] + lnb_ref[...])
    put(2, _sigmoid(proj(5) + bg_ref[0:1, :]))
    put(3, _sigmoid(proj(6) + bg_ref[1:2, :]))
    put_heads(0, proj(2) * Q_SCALE)
    put_heads(1, proj(3))
    put_heads(2, proj(4))


def _inproj(x2, g, w_in, ln_g, ln_b, b_gate, tm):
    t = x2.shape[0]
    const = lambda shape, **kw: pl.BlockSpec(shape, lambda i: (0,) * len(shape), **kw)
    return pl.pallas_call(
        _inproj_kernel,
        grid=(t // tm,),
        in_specs=[
            pl.BlockSpec((tm, D_MODEL), lambda i: (i, 0)),
            const((1, D_MODEL)),
            const((D_MODEL, N_IN_BLOCKS * D_MODEL), pipeline_mode=pl.Buffered(1)),
            const((1, D_MODEL)),
            const((1, D_MODEL)),
            const((2, D_MODEL)),
        ],
        out_specs=[
            pl.BlockSpec((tm, 4 * D_MODEL), lambda i: (i, 0)),
            pl.BlockSpec((3, HEADS // 2, tm, 2 * HEAD_DIM), lambda i: (0, 0, i, 0)),
        ],
        out_shape=[
            jax.ShapeDtypeStruct((t, 4 * D_MODEL), BF16),
            jax.ShapeDtypeStruct((3, HEADS // 2, t, 2 * HEAD_DIM), BF16),
        ],
        compiler_params=pltpu.CompilerParams(
            dimension_semantics=("arbitrary",),
            vmem_limit_bytes=VMEM_LIMIT_BYTES),
        name="inproj",
    )(x2, g, w_in, ln_g, ln_b, b_gate)


N_PAT = len(PATTERNS)
DIL4, DIL16 = PATTERNS[1][1], PATTERNS[2][1]
assert PATTERNS[0][1] == 1 and DIL16 == DIL4 * DIL4
SEG4 = SEQ // DIL4
HP_PER_STEP = 2


def _block_starts():
    general, first = [], []
    for p, (_, dil) in enumerate(PATTERNS):
        seg = SEQ // dil
        for r in range(dil):
            for n in range(seg // BLK):
                (first if n == 0 else general).append((p, p * SEQ + r * seg + n * BLK))
    return general, first


def _attn_kernel(slope_ref, q_ref, k_ref, v_ref, o_ref, *scratch):
    for i in range(HP_PER_STEP):
        _attn_head_pair(slope_ref.at[i], q_ref.at[i], k_ref.at[i], v_ref.at[i], o_ref.at[i],
                        *scratch)


def _attn_head_pair(slope_ref, q_ref, k_ref, v_ref, o_ref,
                    xf, l4, q_s, kh_s, vh_s, ee_s, bias_g, bias_f, num_s, den_s, mx_s):
    lane = lax.broadcasted_iota(jnp.int32, (BLK, 2 * HEAD_DIM), 1)
    head0 = lane < HEAD_DIM

    top_half = lax.broadcasted_iota(jnp.int32, (2 * HEAD_DIM, BLK), 0) < HEAD_DIM

    def emit(dst_rows, val, kind):
        if kind == 0:
            q_s[dst_rows, :] = val.astype(BF16)
        elif kind == 1:
            kt = val.T
            kh_s[0, :, dst_rows] = jnp.where(top_half, kt, 0.0).astype(BF16)
            kh_s[1, :, dst_rows] = jnp.where(top_half, 0.0, kt).astype(BF16)
        else:
            vh_s[0, dst_rows, :] = jnp.where(head0, val, 0.0).astype(BF16)
            vh_s[1, dst_rows, :] = jnp.where(head0, 0.0, val).astype(BF16)

    for kind, src in enumerate((q_ref, k_ref, v_ref)):
        for c in range(SEQ // BLK):
            rows = pl.ds(c * BLK, BLK)
            x = src[rows, :].astype(F32)
            xf[rows, :] = x
            emit(rows, x, kind)
        for r in range(DIL4):
            for a in range(SEG4 // BLK):
                x = xf[pl.ds(DIL4 * BLK * a + r, BLK, stride=DIL4), :]
                dst = r * SEG4 + a * BLK
                l4[pl.ds(dst, BLK), :] = x
                emit(pl.ds(SEQ + dst, BLK), x, kind)
        for c in range(DIL16):
            x = l4[pl.ds((c % DIL4) * SEG4 + c // DIL4, BLK, stride=DIL4), :]
            emit(pl.ds(2 * SEQ + c * BLK, BLK), x, kind)

    e0 = jnp.where(head0, 1.0, 0.0).astype(BF16)
    e1 = jnp.where(head0, 0.0, 1.0).astype(BF16)
    for i in range(2):
        ee_s[0, pl.ds(i * BLK, BLK), :] = e0
        ee_s[1, pl.ds(i * BLK, BLK), :] = e1

    qi = lax.broadcasted_iota(jnp.int32, (BLK, BLK), 0)
    kj = lax.broadcasted_iota(jnp.int32, (BLK, BLK), 1)
    d_cur = (qi - kj).astype(F32)
    ok_cur = kj <= qi
    ok_prev = kj >= qi
    for p, (_, dil) in enumerate(PATTERNS):
        for e in range(2):
            sl = slope_ref[e:e + 1, :]
            sl = sl * LOG2E
            b_cur = jnp.where(ok_cur, -sl * (dil * d_cur), NEG)
            bias_f[p, :, e * BLK:(e + 1) * BLK] = b_cur
            if p < N_PAT - 1:
                b_prev = jnp.where(ok_prev, -sl * (dil * (d_cur + BLK)), NEG)
                bias_g[p, :, 2 * e * BLK:(2 * e + 1) * BLK] = b_prev
                bias_g[p, :, (2 * e + 1) * BLK:(2 * e + 2) * BLK] = b_cur

    def block(p, start, with_prev):
        nk = 2 * BLK if with_prev else BLK
        keys = pl.ds(start - BLK, nk) if with_prev else pl.ds(start, nk)
        q = q_s[pl.ds(start, BLK), :]
        kcat = jnp.concatenate([kh_s[0, :, keys], kh_s[1, :, keys]], axis=1)
        b = bias_g[p] if with_prev else bias_f[p]
        s = jnp.dot(q, kcat, preferred_element_type=F32) + b
        m0 = jnp.max(s[:, :nk], axis=-1, keepdims=True)
        m1 = jnp.max(s[:, nk:], axis=-1, keepdims=True)
        pr = jnp.concatenate([jnp.exp2(s[:, :nk] - m0), jnp.exp2(s[:, nk:] - m1)],
                             axis=1).astype(BF16)
        vcat = jnp.concatenate(
            [jnp.concatenate([vh_s[0, keys, :], vh_s[1, keys, :]], axis=0),
             jnp.concatenate([ee_s[0, pl.ds(0, nk), :], ee_s[1, pl.ds(0, nk), :]], axis=0)],
            axis=1)
        o = jnp.dot(pr, vcat, preferred_element_type=F32)
        if p == N_PAT - 1:
            c = (start - p * SEQ) // BLK
            rows = pl.ds(p * SEQ + (c % DIL4) * SEG4 + c // DIL4, BLK, stride=DIL4)
        else:
            rows = pl.ds(start, BLK)
        num_s[rows, :] = o[:, :2 * HEAD_DIM]
        den_s[rows, :] = o[:, 2 * HEAD_DIM:]
        mx_s[rows, :] = jnp.where(head0, m0, m1)

    general, first = _block_starts()
    for p, start in first:
        block(p, start, False)
    for p, start in general:
        block(p, start, True)

    for r in range(DIL4):
        for a in range(SEG4 // BLK):
            nat = pl.ds(DIL4 * BLK * a + r, BLK, stride=DIL4)
            r4 = r * SEG4 + a * BLK
            p1, p2 = pl.ds(SEQ + r4, BLK), pl.ds(2 * SEQ + r4, BLK)
            x0, x1, x2 = mx_s[nat, :], mx_s[p1, :], mx_s[p2, :]
            m_all = jnp.maximum(jnp.maximum(x0, x1), x2)
            w0, w1, w2 = jnp.exp2(x0 - m_all), jnp.exp2(x1 - m_all), jnp.exp2(x2 - m_all)
            num = w0 * num_s[nat, :] + w1 * num_s[p1, :] + w2 * num_s[p2, :]
            den = w0 * den_s[nat, :] + w1 * den_s[p1, :] + w2 * den_s[p2, :]
            xf[nat, :] = num / den
    o_ref[...] = xf[...].astype(BF16)


def _attention(zq, slopes):
    t = zq.shape[2]
    nb = t // SEQ
    col = 2 * HEAD_DIM
    qkv_spec = lambda kind: pl.BlockSpec((None, HP_PER_STEP, SEQ, col),
                                         lambda b, h, kind=kind: (kind, h, b, 0))
    f32_seq = lambda: pltpu.VMEM((SEQ, col), F32)
    f32_all = lambda: pltpu.VMEM((N_PAT * SEQ, col), F32)
    return pl.pallas_call(
        _attn_kernel,
        grid=(nb, HEADS // 2 // HP_PER_STEP),
        in_specs=[
            pl.BlockSpec((HP_PER_STEP, 2, col), lambda b, h: (h, 0, 0)),
            qkv_spec(0), qkv_spec(1), qkv_spec(2),
        ],
        out_specs=pl.BlockSpec((HP_PER_STEP, SEQ, col), lambda b, h: (h, b, 0)),
        out_shape=jax.ShapeDtypeStruct((HEADS // 2, t, col), BF16),
        scratch_shapes=[
            f32_seq(), f32_seq(),
            pltpu.VMEM((N_PAT * SEQ, col), BF16),
            pltpu.VMEM((2, col, N_PAT * SEQ), BF16),
            pltpu.VMEM((2, N_PAT * SEQ, col), BF16),
            pltpu.VMEM((2, 2 * BLK, col), BF16),
            pltpu.VMEM((N_PAT - 1, BLK, 4 * BLK), F32),
            pltpu.VMEM((N_PAT, BLK, 2 * BLK), F32),
            f32_all(), f32_all(), f32_all(),
        ],
        compiler_params=pltpu.CompilerParams(
            dimension_semantics=("arbitrary", "arbitrary"),
            vmem_limit_bytes=VMEM_LIMIT_BYTES),
        name="dilated_attn",
    )(slopes, zq, zq, zq)


def _merge_kernel(u_ref, v_ref, ga_ref, gb_ref, yb_ref, x_ref, ws_ref, bs_ref,
                  wa_ref, wb_ref, wo_ref, g_ref, o_ref, ya_ref):
    tm = u_ref.shape[0]
    ti = lax.broadcasted_iota(jnp.int32, (CHUNK, CHUNK), 0)
    si = lax.broadcasted_iota(jnp.int32, (CHUNK, CHUNK), 1)
    causal = si <= ti
    for g in range(GROUPS):
        w = jnp.where(causal, ws_ref[g], 0.0).astype(BF16)
        bcol = bs_ref[:, g:g + 1]
        cols = slice(g * GROUP_DIM, (g + 1) * GROUP_DIM)
        for c in range(tm // CHUNK):
            rows = slice(c * CHUNK, (c + 1) * CHUNK)
            mixed = jnp.dot(w, v_ref[rows, cols], preferred_element_type=F32) + bcol
            ya_ref[rows, cols] = (u_ref[rows, cols].astype(F32) * mixed).astype(BF16)
    a = jnp.dot(ya_ref[...], wa_ref[...].astype(BF16), preferred_element_type=F32)
    yb = jnp.concatenate([yb_ref[hp] for hp in range(HEADS // 2)], axis=1)
    b = jnp.dot(yb, wb_ref[...].astype(BF16), preferred_element_type=F32)
    merged = ga_ref[...].astype(F32) * a + gb_ref[...].astype(F32) * b
    y = jnp.dot(merged.astype(BF16), wo_ref[...].astype(BF16), preferred_element_type=F32)
    o_ref[...] = x_ref[...] + _rms_norm(y, g_ref[...])


def _merge(z, yb, x2, w_s, b_s_t, w_a, w_b, w_o, g, tm):
    t = x2.shape[0]
    row = lambda c: pl.BlockSpec((tm, D_MODEL), lambda i, c=c: (i, c))
    const = lambda shape, **kw: pl.BlockSpec(shape, lambda i: (0,) * len(shape), **kw)
    weight = lambda: const((D_MODEL, D_MODEL), pipeline_mode=pl.Buffered(1))
    return pl.pallas_call(
        _merge_kernel,
        grid=(t // tm,),
        in_specs=[row(0), row(1), row(2), row(3),
                  pl.BlockSpec((HEADS // 2, tm, 2 * HEAD_DIM), lambda i: (0, i, 0)), row(0),
                  const((GROUPS, CHUNK, CHUNK)), const((CHUNK, GROUPS)),
                  weight(), weight(), weight(), const((1, D_MODEL))],
        out_specs=row(0),
        out_shape=jax.ShapeDtypeStruct((t, D_MODEL), F32),
        scratch_shapes=[pltpu.VMEM((tm, D_MODEL), BF16)],
        compiler_params=pltpu.CompilerParams(
            dimension_semantics=("arbitrary",),
            vmem_limit_bytes=VMEM_LIMIT_BYTES),
        name="merge",
    )(z, z, z, z, yb, x2, w_s, b_s_t, w_a, w_b, w_o, g)


def _ffn_kernel(x_ref, gpre_ref, w1_ref, w2_ref, gpost_ref, o_ref):
    x = x_ref[...]
    h = _rms_norm(x, gpre_ref[...]).astype(BF16)
    y = None
    for j in range(D_FF // D_MODEL):
        slab = slice(j * D_MODEL, (j + 1) * D_MODEL)
        f = jnp.dot(h, w1_ref[:, slab].astype(BF16), preferred_element_type=F32)
        f = jnp.square(jnp.maximum(f, 0.0)).astype(BF16)
        part = jnp.dot(f, w2_ref[slab, :].astype(BF16), preferred_element_type=F32)
        y = part if y is None else y + part
    o_ref[...] = x + _rms_norm(y, gpost_ref[...])


def _ffn(x1, g_pre, w1, w2, g_post, tm):
    t = x1.shape[0]
    const = lambda shape, **kw: pl.BlockSpec(shape, lambda i: (0,) * len(shape), **kw)
    return pl.pallas_call(
        _ffn_kernel,
        grid=(t // tm,),
        in_specs=[
            pl.BlockSpec((tm, D_MODEL), lambda i: (i, 0)),
            const((1, D_MODEL)),
            const((D_MODEL, D_FF), pipeline_mode=pl.Buffered(1)),
            const((D_FF, D_MODEL), pipeline_mode=pl.Buffered(1)),
            const((1, D_MODEL)),
        ],
        out_specs=pl.BlockSpec((tm, D_MODEL), lambda i: (i, 0)),
        out_shape=jax.ShapeDtypeStruct((t, D_MODEL), F32),
        compiler_params=pltpu.CompilerParams(
            dimension_semantics=("arbitrary",),
            vmem_limit_bytes=VMEM_LIMIT_BYTES),
        name="ffn",
    )(x1, g_pre, w1, w2, g_post)


def kernel(x, norm_mix_pre, w_in, b_gate, ln_v_g, ln_v_b, w_s, b_s, w_a_proj, w_b_proj,
           w_out, norm_mix_post, norm_ffn_pre, w_ff1, w_ff2, norm_ffn_post):
    bsz, s, d = x.shape
    assert (s, d) == (SEQ, D_MODEL)
    depth = w_in.shape[0]
    slopes = jnp.exp2(-8.0 * jnp.arange(1, HEADS + 1, dtype=F32) / HEADS)
    slopes = jnp.broadcast_to(slopes.reshape(HEADS // 2, 2, 1), (HEADS // 2, 2, 2 * HEAD_DIM))
    x2 = x.reshape(bsz * s, d)
    for l in range(depth):
        z, zq = _inproj(x2, norm_mix_pre[l][None], w_in[l], ln_v_g[l][None],
                        ln_v_b[l][None], b_gate[l], tm=512)
        yb = _attention(zq, slopes)
        x2 = _merge(z, yb, x2, w_s[l], b_s[l].T, w_a_proj[l], w_b_proj[l], w_out[l],
                    norm_mix_post[l][None], tm=512)
        x2 = _ffn(x2, norm_ffn_pre[l][None], w_ff1[l], w_ff2[l], norm_ffn_post[l][None],
                  tm=512)
    return x2.reshape(bsz, s, d)
```

```python
import math

import jax
import jax.numpy as jnp
from jax import lax
from jax.experimental import pallas as pl
from jax.experimental.pallas import tpu as pltpu

D_MODEL = 1024
SEQ = 2048
CHUNK = 128
GROUPS = 8
GROUP_DIM = D_MODEL // GROUPS
HEADS = 16
HEAD_DIM = D_MODEL // HEADS
PATTERNS = ((128, 1), (512, 4), (2048, 16))
BLK = 128
D_FF = 4 * D_MODEL
EPS = 1e-6
N_IN_BLOCKS = 7
NEG = -1e30
LOG2E = math.log2(math.e)
Q_SCALE = LOG2E / math.sqrt(HEAD_DIM)

VMEM_LIMIT_BYTES = 56 * 1024 * 1024

F32 = jnp.float32
BF16 = jnp.bfloat16


def _rms_norm(x, g):
    return x * lax.rsqrt(jnp.mean(x * x, axis=-1, keepdims=True) + EPS) * g


def _sigmoid(x):
    return 0.5 * jnp.tanh(0.5 * x) + 0.5


def _inproj_kernel(x_ref, g_ref, w_ref, lng_ref, lnb_ref, bg_ref, z_ref, zq_ref):
    h = _rms_norm(x_ref[...], g_ref[...]).astype(BF16)

    def proj(j):
        return jnp.dot(h, w_ref[:, j * D_MODEL:(j + 1) * D_MODEL].astype(BF16),
                       preferred_element_type=F32)

    def put(c, val):
        z_ref[:, c * D_MODEL:(c + 1) * D_MODEL] = val.astype(BF16)

    def put_heads(kind, val):
        val = val.astype(BF16)
        for hp in range(HEADS // 2):
            zq_ref[kind, hp] = val[:, hp * 2 * HEAD_DIM:(hp + 1) * 2 * HEAD_DIM]

    put(0, jax.nn.gelu(proj(0)))
    v = jax.nn.gelu(proj(1))
    mu = jnp.mean(v, axis=-1, keepdims=True)
    var = jnp.mean(jnp.square(v - mu), axis=-1, keepdims=True)
    put(1, (v - mu) * lax.rsqrt(var + EPS) * lng_ref[...] + lnb_ref[...])
    put(2, _sigmoid(proj(5) + bg_ref[0:1, :]))
    put(3, _sigmoid(proj(6) + bg_ref[1:2, :]))
    put_heads(0, proj(2) * Q_SCALE)
    put_heads(1, proj(3))
    put_heads(2, proj(4))


def _inproj(x2, g, w_in, ln_g, ln_b, b_gate, tm):
    t = x2.shape[0]
    const = lambda shape, **kw: pl.BlockSpec(shape, lambda i: (0,) * len(shape), **kw)
    return pl.pallas_call(
        _inproj_kernel,
        grid=(t // tm,),
        in_specs=[
            pl.BlockSpec((tm, D_MODEL), lambda i: (i, 0)),
            const((1, D_MODEL)),
            const((D_MODEL, N_IN_BLOCKS * D_MODEL), pipeline_mode=pl.Buffered(1)),
            const((1, D_MODEL)),
            const((1, D_MODEL)),
            const((2, D_MODEL)),
        ],
        out_specs=[
            pl.BlockSpec((tm, 4 * D_MODEL), lambda i: (i, 0)),
            pl.BlockSpec((3, HEADS // 2, tm, 2 * HEAD_DIM), lambda i: (0, 0, i, 0)),
        ],
        out_shape=[
            jax.ShapeDtypeStruct((t, 4 * D_MODEL), BF16),
            jax.ShapeDtypeStruct((3, HEADS // 2, t, 2 * HEAD_DIM), BF16),
        ],
        compiler_params=pltpu.CompilerParams(
            dimension_semantics=("arbitrary",),
            vmem_limit_bytes=VMEM_LIMIT_BYTES),
        name="inproj",
    )(x2, g, w_in, ln_g, ln_b, b_gate)


N_PAT = len(PATTERNS)
DIL4, DIL16 = PATTERNS[1][1], PATTERNS[2][1]
assert PATTERNS[0][1] == 1 and DIL16 == DIL4 * DIL4
SEG4 = SEQ // DIL4
HP_PER_STEP = 2


def _block_starts():
    general, first = [], []
    for p, (_, dil) in enumerate(PATTERNS):
        seg = SEQ // dil
        for r in range(dil):
            for n in range(seg // BLK):
                (first if n == 0 else general).append((p, p * SEQ + r * seg + n * BLK))
    return general, first


def _attn_kernel(slope_ref, q_ref, k_ref, v_ref, o_ref, *scratch):
    for i in range(HP_PER_STEP):
        _attn_head_pair(slope_ref.at[i], q_ref.at[i], k_ref.at[i], v_ref.at[i], o_ref.at[i],
                        *scratch)


def _attn_head_pair(slope_ref, q_ref, k_ref, v_ref, o_ref,
                    xf, l4, q_s, kh_s, vh_s, ee_s, bias_g, bias_f, num_s, den_s, mx_s):
    lane = lax.broadcasted_iota(jnp.int32, (BLK, 2 * HEAD_DIM), 1)
    head0 = lane < HEAD_DIM

    def emit(dst_rows, val, kind):
        if kind == 0:
            q_s[dst_rows, :] = val.astype(BF16)
        else:
            dst = kh_s if kind == 1 else vh_s
            dst[0, dst_rows, :] = jnp.where(head0, val, 0.0).astype(BF16)
            dst[1, dst_rows, :] = jnp.where(head0, 0.0, val).astype(BF16)

    for kind, src in enumerate((q_ref, k_ref, v_ref)):
        for c in range(SEQ // BLK):
            rows = pl.ds(c * BLK, BLK)
            x = src[rows, :].astype(F32)
            xf[rows, :] = x
            emit(rows, x, kind)
        for r in range(DIL4):
            for a in range(SEG4 // BLK):
                x = xf[pl.ds(DIL4 * BLK * a + r, BLK, stride=DIL4), :]
                dst = r * SEG4 + a * BLK
                l4[pl.ds(dst, BLK), :] = x
                emit(pl.ds(SEQ + dst, BLK), x, kind)
        for c in range(DIL16):
            x = l4[pl.ds((c % DIL4) * SEG4 + c // DIL4, BLK, stride=DIL4), :]
            emit(pl.ds(2 * SEQ + c * BLK, BLK), x, kind)

    e0 = jnp.where(head0, 1.0, 0.0).astype(BF16)
    e1 = jnp.where(head0, 0.0, 1.0).astype(BF16)
    for i in range(2):
        ee_s[0, pl.ds(i * BLK, BLK), :] = e0
        ee_s[1, pl.ds(i * BLK, BLK), :] = e1

    qi = lax.broadcasted_iota(jnp.int32, (BLK, BLK), 0)
    kj = lax.broadcasted_iota(jnp.int32, (BLK, BLK), 1)
    d_cur = (qi - kj).astype(F32)
    ok_cur = kj <= qi
    ok_prev = kj >= qi
    for p, (_, dil) in enumerate(PATTERNS):
        for e in range(2):
            sl = slope_ref[e:e + 1, :]
            sl = sl * LOG2E
            b_cur = jnp.where(ok_cur, -sl * (dil * d_cur), NEG)
            bias_f[p, :, e * BLK:(e + 1) * BLK] = b_cur
            if p < N_PAT - 1:
                b_prev = jnp.where(ok_prev, -sl * (dil * (d_cur + BLK)), NEG)
                bias_g[p, :, 2 * e * BLK:(2 * e + 1) * BLK] = b_prev
                bias_g[p, :, (2 * e + 1) * BLK:(2 * e + 2) * BLK] = b_cur

    def block(p, start, with_prev):
        nk = 2 * BLK if with_prev else BLK
        keys = pl.ds(start - BLK, nk) if with_prev else pl.ds(start, nk)
        q = q_s[pl.ds(start, BLK), :]
        kcat = jnp.concatenate([kh_s[0, keys, :], kh_s[1, keys, :]], axis=0)
        b = bias_g[p] if with_prev else bias_f[p]
        s = lax.dot_general(q, kcat, (((1,), (1,)), ((), ())),
                            preferred_element_type=F32) + b
        m0 = jnp.max(s[:, :nk], axis=-1, keepdims=True)
        m1 = jnp.max(s[:, nk:], axis=-1, keepdims=True)
        pr = jnp.concatenate([jnp.exp2(s[:, :nk] - m0), jnp.exp2(s[:, nk:] - m1)],
                             axis=1).astype(BF16)
        vcat = jnp.concatenate(
            [jnp.concatenate([vh_s[0, keys, :], vh_s[1, keys, :]], axis=0),
             jnp.concatenate([ee_s[0, pl.ds(0, nk), :], ee_s[1, pl.ds(0, nk), :]], axis=0)],
            axis=1)
        o = jnp.dot(pr, vcat, preferred_element_type=F32)
        if p == N_PAT - 1:
            c = (start - p * SEQ) // BLK
            rows = pl.ds(p * SEQ + (c % DIL4) * SEG4 + c // DIL4, BLK, stride=DIL4)
        else:
            rows = pl.ds(start, BLK)
        num_s[rows, :] = o[:, :2 * HEAD_DIM]
        den_s[rows, :] = o[:, 2 * HEAD_DIM:]
        mx_s[rows, :] = jnp.where(head0, m0, m1)

    general, first = _block_starts()
    for p, start in first:
        block(p, start, False)
    for p, start in general:
        block(p, start, True)

    for r in range(DIL4):
        for a in range(SEG4 // BLK):
            nat = pl.ds(DIL4 * BLK * a + r, BLK, stride=DIL4)
            r4 = r * SEG4 + a * BLK
            p1, p2 = pl.ds(SEQ + r4, BLK), pl.ds(2 * SEQ + r4, BLK)
            x0, x1, x2 = mx_s[nat, :], mx_s[p1, :], mx_s[p2, :]
            m_all = jnp.maximum(jnp.maximum(x0, x1), x2)
            w0, w1, w2 = jnp.exp2(x0 - m_all), jnp.exp2(x1 - m_all), jnp.exp2(x2 - m_all)
            num = w0 * num_s[nat, :] + w1 * num_s[p1, :] + w2 * num_s[p2, :]
            den = w0 * den_s[nat, :] + w1 * den_s[p1, :] + w2 * den_s[p2, :]
            xf[nat, :] = num / den
    o_ref[...] = xf[...].astype(BF16)


def _attention(zq, slopes):
    t = zq.shape[2]
    nb = t // SEQ
    col = 2 * HEAD_DIM
    qkv_spec = lambda kind: pl.BlockSpec((None, HP_PER_STEP, SEQ, col),
                                         lambda b, h, kind=kind: (kind, h, b, 0))
    f32_seq = lambda: pltpu.VMEM((SEQ, col), F32)
    f32_all = lambda: pltpu.VMEM((N_PAT * SEQ, col), F32)
    return pl.pallas_call(
        _attn_kernel,
        grid=(nb, HEADS // 2 // HP_PER_STEP),
        in_specs=[
            pl.BlockSpec((HP_PER_STEP, 2, col), lambda b, h: (h, 0, 0)),
            qkv_spec(0), qkv_spec(1), qkv_spec(2),
        ],
        out_specs=pl.BlockSpec((HP_PER_STEP, SEQ, col), lambda b, h: (h, b, 0)),
        out_shape=jax.ShapeDtypeStruct((HEADS // 2, t, col), BF16),
        scratch_shapes=[
            f32_seq(), f32_seq(),
            pltpu.VMEM((N_PAT * SEQ, col), BF16),
            pltpu.VMEM((2, N_PAT * SEQ, col), BF16),
            pltpu.VMEM((2, N_PAT * SEQ, col), BF16),
            pltpu.VMEM((2, 2 * BLK, col), BF16),
            pltpu.VMEM((N_PAT - 1, BLK, 4 * BLK), F32),
            pltpu.VMEM((N_PAT, BLK, 2 * BLK), F32),
            f32_all(), f32_all(), f32_all(),
        ],
        compiler_params=pltpu.CompilerParams(
            dimension_semantics=("arbitrary", "arbitrary"),
            vmem_limit_bytes=VMEM_LIMIT_BYTES),
        name="dilated_attn",
    )(slopes, zq, zq, zq)


def _merge_kernel(u_ref, v_ref, ga_ref, gb_ref, yb_ref, x_ref, ws_ref, bs_ref,
                  wa_ref, wb_ref, wo_ref, g_ref, o_ref, ya_ref):
    tm = u_ref.shape[0]
    ti = lax.broadcasted_iota(jnp.int32, (CHUNK, CHUNK), 0)
    si = lax.broadcasted_iota(jnp.int32, (CHUNK, CHUNK), 1)
    causal = si <= ti
    for g in range(GROUPS):
        w = jnp.where(causal, ws_ref[g], 0.0).astype(BF16)
        bcol = bs_ref[:, g:g + 1]
        cols = slice(g * GROUP_DIM, (g + 1) * GROUP_DIM)
        for c in range(tm // CHUNK):
            rows = slice(c * CHUNK, (c + 1) * CHUNK)
            mixed = jnp.dot(w, v_ref[rows, cols], preferred_element_type=F32) + bcol
            ya_ref[rows, cols] = (u_ref[rows, cols].astype(F32) * mixed).astype(BF16)
    a = jnp.dot(ya_ref[...], wa_ref[...].astype(BF16), preferred_element_type=F32)
    yb = jnp.concatenate([yb_ref[hp] for hp in range(HEADS // 2)], axis=1)
    b = jnp.dot(yb, wb_ref[...].astype(BF16), preferred_element_type=F32)
    merged = ga_ref[...].astype(F32) * a + gb_ref[...].astype(F32) * b
    y = jnp.dot(merged.astype(BF16), wo_ref[...].astype(BF16), preferred_element_type=F32)
    o_ref[...] = x_ref[...] + _rms_norm(y, g_ref[...])


def _merge(z, yb, x2, w_s, b_s_t, w_a, w_b, w_o, g, tm):
    t = x2.shape[0]
    row = lambda c: pl.BlockSpec((tm, D_MODEL), lambda i, c=c: (i, c))
    const = lambda shape, **kw: pl.BlockSpec(shape, lambda i: (0,) * len(shape), **kw)
    weight = lambda: const((D_MODEL, D_MODEL), pipeline_mode=pl.Buffered(1))
    return pl.pallas_call(
        _merge_kernel,
        grid=(t // tm,),
        in_specs=[row(0), row(1), row(2), row(3),
                  pl.BlockSpec((HEADS // 2, tm, 2 * HEAD_DIM), lambda i: (0, i, 0)), row(0),
                  const((GROUPS, CHUNK, CHUNK)), const((CHUNK, GROUPS)),
                  weight(), weight(), weight(), const((1, D_MODEL))],
        out_specs=row(0),
        out_shape=jax.ShapeDtypeStruct((t, D_MODEL), F32),
        scratch_shapes=[pltpu.VMEM((tm, D_MODEL), BF16)],
        compiler_params=pltpu.CompilerParams(
            dimension_semantics=("arbitrary",),
            vmem_limit_bytes=VMEM_LIMIT_BYTES),
        name="merge",
    )(z, z, z, z, yb, x2, w_s, b_s_t, w_a, w_b, w_o, g)


def _ffn_kernel(x_ref, gpre_ref, w1_ref, w2_ref, gpost_ref, o_ref):
    x = x_ref[...]
    h = _rms_norm(x, gpre_ref[...]).astype(BF16)
    y = None
    for j in range(D_FF // D_MODEL):
        slab = slice(j * D_MODEL, (j + 1) * D_MODEL)
        f = jnp.dot(h, w1_ref[:, slab].astype(BF16), preferred_element_type=F32)
        f = jnp.square(jnp.maximum(f, 0.0)).astype(BF16)
        part = jnp.dot(f, w2_ref[slab, :].astype(BF16), preferred_element_type=F32)
        y = part if y is None else y + part
    o_ref[...] = x + _rms_norm(y, gpost_ref[...])


def _ffn(x1, g_pre, w1, w2, g_post, tm):
    t = x1.shape[0]
    const = lambda shape, **kw: pl.BlockSpec(shape, lambda i: (0,) * len(shape), **kw)
    return pl.pallas_call(
        _ffn_kernel,
        grid=(t // tm,),
        in_specs=[
            pl.BlockSpec((tm, D_MODEL), lambda i: (i, 0)),
            const((1, D_MODEL)),
            const((D_MODEL, D_FF), pipeline_mode=pl.Buffered(1)),
            const((D_FF, D_MODEL), pipeline_mode=pl.Buffered(1)),
            const((1, D_MODEL)),
        ],
        out_specs=pl.BlockSpec((tm, D_MODEL), lambda i: (i, 0)),
        out_shape=jax.ShapeDtypeStruct((t, D_MODEL), F32),
        compiler_params=pltpu.CompilerParams(
            dimension_semantics=("arbitrary",),
            vmem_limit_bytes=VMEM_LIMIT_BYTES),
        name="ffn",
    )(x1, g_pre, w1, w2, g_post)


def kernel(x, norm_mix_pre, w_in, b_gate, ln_v_g, ln_v_b, w_s, b_s, w_a_proj, w_b_proj,
           w_out, norm_mix_post, norm_ffn_pre, w_ff1, w_ff2, norm_ffn_post):
    bsz, s, d = x.shape
    assert (s, d) == (SEQ, D_MODEL)
    depth = w_in.shape[0]
    slopes = jnp.exp2(-8.0 * jnp.arange(1, HEADS + 1, dtype=F32) / HEADS)
    slopes = jnp.broadcast_to(slopes.reshape(HEADS // 2, 2, 1), (HEADS // 2, 2, 2 * HEAD_DIM))
    x2 = x.reshape(bsz * s, d)
    for l in range(depth):
        z, zq = _inproj(x2, norm_mix_pre[l][None], w_in[l], ln_v_g[l][None],
                        ln_v_b[l][None], b_gate[l], tm=512)
        yb = _attention(zq, slopes)
        x2 = _merge(z, yb, x2, w_s[l], b_s[l].T, w_a_proj[l], w_b_proj[l], w_out[l],
                    norm_mix_post[l][None], tm=512)
        x2 = _ffn(x2, norm_ffn_pre[l][None], w_ff1[l], w_ff2[l], norm_ffn_post[l][None],
                  tm=512)
    return x2.reshape(bsz, s, d)
```

```python
import math

import jax
import jax.numpy as jnp
from jax import lax
from jax.experimental import pallas as pl
from jax.experimental.pallas import tpu as pltpu

D_MODEL = 1024
SEQ = 2048
CHUNK = 128
GROUPS = 8
GROUP_DIM = D_MODEL // GROUPS
HEADS = 16
HEAD_DIM = D_MODEL // HEADS
PATTERNS = ((128, 1), (512, 4), (2048, 16))
BLK = 128
D_FF = 4 * D_MODEL
EPS = 1e-6
N_IN_BLOCKS = 7
NEG = -1e30
LOG2E = math.log2(math.e)
Q_SCALE = LOG2E / math.sqrt(HEAD_DIM)

VMEM_LIMIT_BYTES = 56 * 1024 * 1024

F32 = jnp.float32
BF16 = jnp.bfloat16


def _rms_norm(x, g):
    return x * lax.rsqrt(jnp.mean(x * x, axis=-1, keepdims=True) + EPS) * g


def _sigmoid(x):
    return 0.5 * jnp.tanh(0.5 * x) + 0.5


def _inproj_kernel(x_ref, g_ref, w_ref, lng_ref, lnb_ref, bg_ref, z_ref, zq_ref):
    h = _rms_norm(x_ref[...], g_ref[...]).astype(BF16)

    def proj(j):
        return jnp.dot(h, w_ref[:, j * D_MODEL:(j + 1) * D_MODEL].astype(BF16),
                       preferred_element_type=F32)

    def put(c, val):
        z_ref[:, c * D_MODEL:(c + 1) * D_MODEL] = val.astype(BF16)

    def put_heads(kind, val):
        val = val.astype(BF16)
        for hp in range(HEADS // 2):
            zq_ref[kind, hp] = val[:, hp * 2 * HEAD_DIM:(hp + 1) * 2 * HEAD_DIM]

    put(0, jax.nn.gelu(proj(0)))
    v = jax.nn.gelu(proj(1))
    mu = jnp.mean(v, axis=-1, keepdims=True)
    var = jnp.mean(jnp.square(v - mu), axis=-1, keepdims=True)
    put(1, (v - mu) * lax.rsqrt(var + EPS) * lng_ref[...] + lnb_ref[...])
    put(2, _sigmoid(proj(5) + bg_ref[0:1, :]))
    put(3, _sigmoid(proj(6) + bg_ref[1:2, :]))
    put_heads(0, proj(2) * Q_SCALE)
    put_heads(1, proj(3))
    put_heads(2, proj(4))


def _inproj(x2, g, w_in, ln_g, ln_b, b_gate, tm):
    t = x2.shape[0]
    const = lambda shape, **kw: pl.BlockSpec(shape, lambda i: (0,) * len(shape), **kw)
    return pl.pallas_call(
        _inproj_kernel,
        grid=(t // tm,),
        in_specs=[
            pl.BlockSpec((tm, D_MODEL), lambda i: (i, 0)),
            const((1, D_MODEL)),
            const((D_MODEL, N_IN_BLOCKS * D_MODEL), pipeline_mode=pl.Buffered(1)),
            const((1, D_MODEL)),
            const((1, D_MODEL)),
            const((2, D_MODEL)),
        ],
        out_specs=[
            pl.BlockSpec((tm, 4 * D_MODEL), lambda i: (i, 0)),
            pl.BlockSpec((3, HEADS // 2, tm, 2 * HEAD_DIM), lambda i: (0, 0, i, 0)),
        ],
        out_shape=[
            jax.ShapeDtypeStruct((t, 4 * D_MODEL), BF16),
            jax.ShapeDtypeStruct((3, HEADS // 2, t, 2 * HEAD_DIM), BF16),
        ],
        compiler_params=pltpu.CompilerParams(
            dimension_semantics=("arbitrary",),
            vmem_limit_bytes=VMEM_LIMIT_BYTES),
        name="inproj",
    )(x2, g, w_in, ln_g, ln_b, b_gate)


N_PAT = len(PATTERNS)
DIL4, DIL16 = PATTERNS[1][1], PATTERNS[2][1]
assert PATTERNS[0][1] == 1 and DIL16 == DIL4 * DIL4
SEG4 = SEQ // DIL4
HP_PER_STEP = 2


def _block_starts():
    general, first = [], []
    for p, (_, dil) in enumerate(PATTERNS):
        seg = SEQ // dil
        for r in range(dil):
            for n in range(seg // BLK):
                (first if n == 0 else general).append((p, p * SEQ + r * seg + n * BLK))
    return general, first


def _attn_kernel(slope_ref, q_ref, k_ref, v_ref, o_ref, *scratch):
    for i in range(HP_PER_STEP):
        _attn_head_pair(slope_ref.at[i], q_ref.at[i], k_ref.at[i], v_ref.at[i], o_ref.at[i],
                        *scratch)


def _attn_head_pair(slope_ref, q_ref, k_ref, v_ref, o_ref,
                    xf, l4, q_s, kh_s, vh_s, ee_s, bias_g, bias_f, num_s, den_s, mx_s):
    lane = lax.broadcasted_iota(jnp.int32, (BLK, 2 * HEAD_DIM), 1)
    head0 = lane < HEAD_DIM

    def emit(dst_rows, val, kind):
        (q_s, kh_s, vh_s)[kind][dst_rows, :] = val.astype(BF16)

    def per_head(x):
        keep0 = lax.broadcasted_iota(jnp.int32, x.shape, 1) < HEAD_DIM
        zero = jnp.zeros_like(x)
        return jnp.concatenate([jnp.where(keep0, x, zero), jnp.where(keep0, zero, x)], axis=0)

    for kind, src in enumerate((q_ref, k_ref, v_ref)):
        for c in range(SEQ // BLK):
            rows = pl.ds(c * BLK, BLK)
            x = src[rows, :].astype(F32)
            xf[rows, :] = x
            emit(rows, x, kind)
        for r in range(DIL4):
            for a in range(SEG4 // BLK):
                x = xf[pl.ds(DIL4 * BLK * a + r, BLK, stride=DIL4), :]
                dst = r * SEG4 + a * BLK
                l4[pl.ds(dst, BLK), :] = x
                emit(pl.ds(SEQ + dst, BLK), x, kind)
        for c in range(DIL16):
            x = l4[pl.ds((c % DIL4) * SEG4 + c // DIL4, BLK, stride=DIL4), :]
            emit(pl.ds(2 * SEQ + c * BLK, BLK), x, kind)

    e0 = jnp.where(head0, 1.0, 0.0).astype(BF16)
    e1 = jnp.where(head0, 0.0, 1.0).astype(BF16)
    for i in range(2):
        ee_s[0, pl.ds(i * BLK, BLK), :] = e0
        ee_s[1, pl.ds(i * BLK, BLK), :] = e1

    qi = lax.broadcasted_iota(jnp.int32, (BLK, BLK), 0)
    kj = lax.broadcasted_iota(jnp.int32, (BLK, BLK), 1)
    d_cur = (qi - kj).astype(F32)
    ok_cur = kj <= qi
    ok_prev = kj >= qi
    for p, (_, dil) in enumerate(PATTERNS):
        for e in range(2):
            sl = slope_ref[e:e + 1, :]
            sl = sl * LOG2E
            b_cur = jnp.where(ok_cur, -sl * (dil * d_cur), NEG)
            bias_f[p, :, e * BLK:(e + 1) * BLK] = b_cur
            if p < N_PAT - 1:
                b_prev = jnp.where(ok_prev, -sl * (dil * (d_cur + BLK)), NEG)
                bias_g[p, :, 2 * e * BLK:(2 * e + 1) * BLK] = b_prev
                bias_g[p, :, (2 * e + 1) * BLK:(2 * e + 2) * BLK] = b_cur

    def block(p, start, with_prev):
        nk = 2 * BLK if with_prev else BLK
        keys = pl.ds(start - BLK, nk) if with_prev else pl.ds(start, nk)
        q = q_s[pl.ds(start, BLK), :]
        kcat = per_head(kh_s[keys, :])
        b = bias_g[p] if with_prev else bias_f[p]
        s = lax.dot_general(q, kcat, (((1,), (1,)), ((), ())),
                            preferred_element_type=F32) + b
        m0 = jnp.max(s[:, :nk], axis=-1, keepdims=True)
        m1 = jnp.max(s[:, nk:], axis=-1, keepdims=True)
        pr = jnp.concatenate([jnp.exp2(s[:, :nk] - m0), jnp.exp2(s[:, nk:] - m1)],
                             axis=1).astype(BF16)
        vcat = jnp.concatenate(
            [per_head(vh_s[keys, :]),
             jnp.concatenate([ee_s[0, pl.ds(0, nk), :], ee_s[1, pl.ds(0, nk), :]], axis=0)],
            axis=1)
        o = jnp.dot(pr, vcat, preferred_element_type=F32)
        if p == N_PAT - 1:
            c = (start - p * SEQ) // BLK
            rows = pl.ds(p * SEQ + (c % DIL4) * SEG4 + c // DIL4, BLK, stride=DIL4)
        else:
            rows = pl.ds(start, BLK)
        num_s[rows, :] = o[:, :2 * HEAD_DIM]
        den_s[rows, :] = o[:, 2 * HEAD_DIM:]
        mx_s[rows, :] = jnp.where(head0, m0, m1)

    general, first = _block_starts()
    for p, start in first:
        block(p, start, False)
    for p, start in general:
        block(p, start, True)

    for r in range(DIL4):
        for a in range(SEG4 // BLK):
            nat = pl.ds(DIL4 * BLK * a + r, BLK, stride=DIL4)
            r4 = r * SEG4 + a * BLK
            p1, p2 = pl.ds(SEQ + r4, BLK), pl.ds(2 * SEQ + r4, BLK)
            x0, x1, x2 = mx_s[nat, :], mx_s[p1, :], mx_s[p2, :]
            m_all = jnp.maximum(jnp.maximum(x0, x1), x2)
            w0, w1, w2 = jnp.exp2(x0 - m_all), jnp.exp2(x1 - m_all), jnp.exp2(x2 - m_all)
            num = w0 * num_s[nat, :] + w1 * num_s[p1, :] + w2 * num_s[p2, :]
            den = w0 * den_s[nat, :] + w1 * den_s[p1, :] + w2 * den_s[p2, :]
            xf[nat, :] = num / den
    o_ref[...] = xf[...].astype(BF16)


def _attention(zq, slopes):
    t = zq.shape[2]
    nb = t // SEQ
    col = 2 * HEAD_DIM
    qkv_spec = lambda kind: pl.BlockSpec((None, HP_PER_STEP, SEQ, col),
                                         lambda b, h, kind=kind: (kind, h, b, 0))
    f32_seq = lambda: pltpu.VMEM((SEQ, col), F32)
    f32_all = lambda: pltpu.VMEM((N_PAT * SEQ, col), F32)
    return pl.pallas_call(
        _attn_kernel,
        grid=(nb, HEADS // 2 // HP_PER_STEP),
        in_specs=[
            pl.BlockSpec((HP_PER_STEP, 2, col), lambda b, h: (h, 0, 0)),
            qkv_spec(0), qkv_spec(1), qkv_spec(2),
        ],
        out_specs=pl.BlockSpec((HP_PER_STEP, SEQ, col), lambda b, h: (h, b, 0)),
        out_shape=jax.ShapeDtypeStruct((HEADS // 2, t, col), BF16),
        scratch_shapes=[
            f32_seq(), f32_seq(),
            pltpu.VMEM((N_PAT * SEQ, col), BF16),
            pltpu.VMEM((N_PAT * SEQ, col), BF16),
            pltpu.VMEM((N_PAT * SEQ, col), BF16),
            pltpu.VMEM((2, 2 * BLK, col), BF16),
            pltpu.VMEM((N_PAT - 1, BLK, 4 * BLK), F32),
            pltpu.VMEM((N_PAT, BLK, 2 * BLK), F32),
            f32_all(), f32_all(), f32_all(),
        ],
        compiler_params=pltpu.CompilerParams(
            dimension_semantics=("arbitrary", "arbitrary"),
            vmem_limit_bytes=VMEM_LIMIT_BYTES),
        name="dilated_attn",
    )(slopes, zq, zq, zq)


def _merge_kernel(u_ref, v_ref, ga_ref, gb_ref, yb_ref, x_ref, ws_ref, bs_ref,
                  wa_ref, wb_ref, wo_ref, g_ref, o_ref, ya_ref):
    tm = u_ref.shape[0]
    ti = lax.broadcasted_iota(jnp.int32, (CHUNK, CHUNK), 0)
    si = lax.broadcasted_iota(jnp.int32, (CHUNK, CHUNK), 1)
    causal = si <= ti
    for g in range(GROUPS):
        w = jnp.where(causal, ws_ref[g], 0.0).astype(BF16)
        bcol = bs_ref[:, g:g + 1]
        cols = slice(g * GROUP_DIM, (g + 1) * GROUP_DIM)
        for c in range(tm // CHUNK):
            rows = slice(c * CHUNK, (c + 1) * CHUNK)
            mixed = jnp.dot(w, v_ref[rows, cols], preferred_element_type=F32) + bcol
            ya_ref[rows, cols] = (u_ref[rows, cols].astype(F32) * mixed).astype(BF16)
    a = jnp.dot(ya_ref[...], wa_ref[...].astype(BF16), preferred_element_type=F32)
    yb = jnp.concatenate([yb_ref[hp] for hp in range(HEADS // 2)], axis=1)
    b = jnp.dot(yb, wb_ref[...].astype(BF16), preferred_element_type=F32)
    merged = ga_ref[...].astype(F32) * a + gb_ref[...].astype(F32) * b
    y = jnp.dot(merged.astype(BF16), wo_ref[...].astype(BF16), preferred_element_type=F32)
    o_ref[...] = x_ref[...] + _rms_norm(y, g_ref[...])


def _merge(z, yb, x2, w_s, b_s_t, w_a, w_b, w_o, g, tm):
    t = x2.shape[0]
    row = lambda c: pl.BlockSpec((tm, D_MODEL), lambda i, c=c: (i, c))
    const = lambda shape, **kw: pl.BlockSpec(shape, lambda i: (0,) * len(shape), **kw)
    weight = lambda: const((D_MODEL, D_MODEL), pipeline_mode=pl.Buffered(1))
    return pl.pallas_call(
        _merge_kernel,
        grid=(t // tm,),
        in_specs=[row(0), row(1), row(2), row(3),
                  pl.BlockSpec((HEADS // 2, tm, 2 * HEAD_DIM), lambda i: (0, i, 0)), row(0),
                  const((GROUPS, CHUNK, CHUNK)), const((CHUNK, GROUPS)),
                  weight(), weight(), weight(), const((1, D_MODEL))],
        out_specs=row(0),
        out_shape=jax.ShapeDtypeStruct((t, D_MODEL), F32),
        scratch_shapes=[pltpu.VMEM((tm, D_MODEL), BF16)],
        compiler_params=pltpu.CompilerParams(
            dimension_semantics=("arbitrary",),
            vmem_limit_bytes=VMEM_LIMIT_BYTES),
        name="merge",
    )(z, z, z, z, yb, x2, w_s, b_s_t, w_a, w_b, w_o, g)


def _ffn_kernel(x_ref, gpre_ref, w1_ref, w2_ref, gpost_ref, o_ref):
    x = x_ref[...]
    h = _rms_norm(x, gpre_ref[...]).astype(BF16)
    y = None
    for j in range(D_FF // D_MODEL):
        slab = slice(j * D_MODEL, (j + 1) * D_MODEL)
        f = jnp.dot(h, w1_ref[:, slab].astype(BF16), preferred_element_type=F32)
        f = jnp.square(jnp.maximum(f, 0.0)).astype(BF16)
        part = jnp.dot(f, w2_ref[slab, :].astype(BF16), preferred_element_type=F32)
        y = part if y is None else y + part
    o_ref[...] = x + _rms_norm(y, gpost_ref[...])


def _ffn(x1, g_pre, w1, w2, g_post, tm):
    t = x1.shape[0]
    const = lambda shape, **kw: pl.BlockSpec(shape, lambda i: (0,) * len(shape), **kw)
    return pl.pallas_call(
        _ffn_kernel,
        grid=(t // tm,),
        in_specs=[
            pl.BlockSpec((tm, D_MODEL), lambda i: (i, 0)),
            const((1, D_MODEL)),
            const((D_MODEL, D_FF), pipeline_mode=pl.Buffered(1)),
            const((D_FF, D_MODEL), pipeline_mode=pl.Buffered(1)),
            const((1, D_MODEL)),
        ],
        out_specs=pl.BlockSpec((tm, D_MODEL), lambda i: (i, 0)),
        out_shape=jax.ShapeDtypeStruct((t, D_MODEL), F32),
        compiler_params=pltpu.CompilerParams(
            dimension_semantics=("arbitrary",),
            vmem_limit_bytes=VMEM_LIMIT_BYTES),
        name="ffn",
    )(x1, g_pre, w1, w2, g_post)


def kernel(x, norm_mix_pre, w_in, b_gate, ln_v_g, ln_v_b, w_s, b_s, w_a_proj, w_b_proj,
           w_out, norm_mix_post, norm_ffn_pre, w_ff1, w_ff2, norm_ffn_post):
    bsz, s, d = x.shape
    assert (s, d) == (SEQ, D_MODEL)
    depth = w_in.shape[0]
    slopes = jnp.exp2(-8.0 * jnp.arange(1, HEADS + 1, dtype=F32) / HEADS)
    slopes = jnp.broadcast_to(slopes.reshape(HEADS // 2, 2, 1), (HEADS // 2, 2, 2 * HEAD_DIM))
    x2 = x.reshape(bsz * s, d)
    for l in range(depth):
        z, zq = _inproj(x2, norm_mix_pre[l][None], w_in[l], ln_v_g[l][None],
                        ln_v_b[l][None], b_gate[l], tm=512)
        yb = _attention(zq, slopes)
        x2 = _merge(z, yb, x2, w_s[l], b_s[l].T, w_a_proj[l], w_b_proj[l], w_out[l],
                    norm_mix_post[l][None], tm=512)
        x2 = _ffn(x2, norm_ffn_pre[l][None], w_ff1[l], w_ff2[l], norm_ffn_post[l][None],
                  tm=512)
    return x2.reshape(bsz, s, d)
```

```python
import math

import jax
import jax.numpy as jnp
from jax import lax
from jax.experimental import pallas as pl
from jax.experimental.pallas import tpu as pltpu

D_MODEL = 1024
SEQ = 2048
CHUNK = 128
GROUPS = 8
GROUP_DIM = D_MODEL // GROUPS
HEADS = 16
HEAD_DIM = D_MODEL // HEADS
PATTERNS = ((128, 1), (512, 4), (2048, 16))
BLK = 128
D_FF = 4 * D_MODEL
EPS = 1e-6
N_IN_BLOCKS = 7
NEG = -1e30
LOG2E = math.log2(math.e)
Q_SCALE = LOG2E / math.sqrt(HEAD_DIM)

VMEM_LIMIT_BYTES = 56 * 1024 * 1024

F32 = jnp.float32
BF16 = jnp.bfloat16


def _rms_norm(x, g):
    return x * lax.rsqrt(jnp.mean(x * x, axis=-1, keepdims=True) + EPS) * g


def _sigmoid(x):
    return 0.5 * jnp.tanh(0.5 * x) + 0.5


def _inproj_kernel(x_ref, g_ref, w_ref, lng_ref, lnb_ref, bg_ref, z_ref, zq_ref):
    h = _rms_norm(x_ref[...], g_ref[...]).astype(BF16)

    def proj(j):
        return jnp.dot(h, w_ref[:, j * D_MODEL:(j + 1) * D_MODEL].astype(BF16),
                       preferred_element_type=F32)

    def put(c, val):
        z_ref[:, c * D_MODEL:(c + 1) * D_MODEL] = val.astype(BF16)

    def put_heads(kind, val):
        val = val.astype(BF16)
        for hp in range(HEADS // 2):
            zq_ref[kind, hp] = val[:, hp * 2 * HEAD_DIM:(hp + 1) * 2 * HEAD_DIM]

    put(0, jax.nn.gelu(proj(0)))
    v = jax.nn.gelu(proj(1))
    mu = jnp.mean(v, axis=-1, keepdims=True)
    var = jnp.mean(jnp.square(v - mu), axis=-1, keepdims=True)
    put(1, (v - mu) * lax.rsqrt(var + EPS) * lng_ref[...] + lnb_ref[...])
    put(2, _sigmoid(proj(5) + bg_ref[0:1, :]))
    put(3, _sigmoid(proj(6) + bg_ref[1:2, :]))
    put_heads(0, proj(2) * Q_SCALE)
    put_heads(1, proj(3))
    put_heads(2, proj(4))


def _inproj(x2, g, w_in, ln_g, ln_b, b_gate, tm):
    t = x2.shape[0]
    const = lambda shape, **kw: pl.BlockSpec(shape, lambda i: (0,) * len(shape), **kw)
    return pl.pallas_call(
        _inproj_kernel,
        grid=(t // tm,),
        in_specs=[
            pl.BlockSpec((tm, D_MODEL), lambda i: (i, 0)),
            const((1, D_MODEL)),
            const((D_MODEL, N_IN_BLOCKS * D_MODEL), pipeline_mode=pl.Buffered(1)),
            const((1, D_MODEL)),
            const((1, D_MODEL)),
            const((2, D_MODEL)),
        ],
        out_specs=[
            pl.BlockSpec((tm, 4 * D_MODEL), lambda i: (i, 0)),
            pl.BlockSpec((3, HEADS // 2, tm, 2 * HEAD_DIM), lambda i: (0, 0, i, 0)),
        ],
        out_shape=[
            jax.ShapeDtypeStruct((t, 4 * D_MODEL), BF16),
            jax.ShapeDtypeStruct((3, HEADS // 2, t, 2 * HEAD_DIM), BF16),
        ],
        compiler_params=pltpu.CompilerParams(
            dimension_semantics=("arbitrary",),
            vmem_limit_bytes=VMEM_LIMIT_BYTES),
        name="inproj",
    )(x2, g, w_in, ln_g, ln_b, b_gate)


N_PAT = len(PATTERNS)
DIL4, DIL16 = PATTERNS[1][1], PATTERNS[2][1]
assert PATTERNS[0][1] == 1 and DIL16 == DIL4 * DIL4
SEG4 = SEQ // DIL4
HP_PER_STEP = 2


def _block_starts():
    general, first = [], []
    for p, (_, dil) in enumerate(PATTERNS):
        seg = SEQ // dil
        for r in range(dil):
            for n in range(seg // BLK):
                (first if n == 0 else general).append((p, p * SEQ + r * seg + n * BLK))
    return general, first


def _attn_kernel(slope_ref, q_ref, k_ref, v_ref, o_ref, *scratch):
    for i in range(HP_PER_STEP):
        _attn_head_pair(slope_ref.at[i], q_ref.at[i], k_ref.at[i], v_ref.at[i], o_ref.at[i],
                        *scratch)


def _attn_head_pair(slope_ref, q_ref, k_ref, v_ref, o_ref,
                    xf, l4, q_s, kh_s, vh_s, ee_s, bias_g, bias_f, num_s, den_s, mx_s):
    lane = lax.broadcasted_iota(jnp.int32, (BLK, 2 * HEAD_DIM), 1)
    head0 = lane < HEAD_DIM

    def emit(dst_rows, val, kind):
        (q_s, kh_s, vh_s)[kind][dst_rows, :] = val.astype(BF16)

    def per_head(x):
        keep0 = lax.broadcasted_iota(jnp.int32, x.shape, 1) < HEAD_DIM
        zero = jnp.zeros_like(x)
        return jnp.concatenate([jnp.where(keep0, x, zero), jnp.where(keep0, zero, x)], axis=0)

    for kind, src in enumerate((q_ref, k_ref, v_ref)):
        xf[...] = src[...].astype(F32)
        for r in range(DIL4):
            for a in range(SEG4 // BLK):
                x = xf[pl.ds(DIL4 * BLK * a + r, BLK, stride=DIL4), :]
                dst = r * SEG4 + a * BLK
                l4[pl.ds(dst, BLK), :] = x
                emit(pl.ds(SEQ + dst, BLK), x, kind)
        for c in range(DIL16):
            x = l4[pl.ds((c % DIL4) * SEG4 + c // DIL4, BLK, stride=DIL4), :]
            emit(pl.ds(2 * SEQ + c * BLK, BLK), x, kind)

    e0 = jnp.where(head0, 1.0, 0.0).astype(BF16)
    e1 = jnp.where(head0, 0.0, 1.0).astype(BF16)
    for i in range(2):
        ee_s[0, pl.ds(i * BLK, BLK), :] = e0
        ee_s[1, pl.ds(i * BLK, BLK), :] = e1

    qi = lax.broadcasted_iota(jnp.int32, (BLK, BLK), 0)
    kj = lax.broadcasted_iota(jnp.int32, (BLK, BLK), 1)
    d_cur = (qi - kj).astype(F32)
    ok_cur = kj <= qi
    ok_prev = kj >= qi
    for p, (_, dil) in enumerate(PATTERNS):
        for e in range(2):
            sl = slope_ref[e:e + 1, :]
            sl = sl * LOG2E
            b_cur = jnp.where(ok_cur, -sl * (dil * d_cur), NEG)
            bias_f[p, :, e * BLK:(e + 1) * BLK] = b_cur
            if p < N_PAT - 1:
                b_prev = jnp.where(ok_prev, -sl * (dil * (d_cur + BLK)), NEG)
                bias_g[p, :, 2 * e * BLK:(2 * e + 1) * BLK] = b_prev
                bias_g[p, :, (2 * e + 1) * BLK:(2 * e + 2) * BLK] = b_cur

    def block(p, start, with_prev):
        nk = 2 * BLK if with_prev else BLK
        keys = pl.ds(start - BLK, nk) if with_prev else pl.ds(start, nk)
        q_src, k_src, v_src = (q_ref, k_ref, v_ref) if p == 0 else (q_s, kh_s, vh_s)
        q = q_src[pl.ds(start, BLK), :]
        kcat = per_head(k_src[keys, :])
        b = bias_g[p] if with_prev else bias_f[p]
        s = lax.dot_general(q, kcat, (((1,), (1,)), ((), ())),
                            preferred_element_type=F32) + b
        m0 = jnp.max(s[:, :nk], axis=-1, keepdims=True)
        m1 = jnp.max(s[:, nk:], axis=-1, keepdims=True)
        pr = jnp.concatenate([jnp.exp2(s[:, :nk] - m0), jnp.exp2(s[:, nk:] - m1)],
                             axis=1).astype(BF16)
        vcat = jnp.concatenate(
            [per_head(v_src[keys, :]),
             jnp.concatenate([ee_s[0, pl.ds(0, nk), :], ee_s[1, pl.ds(0, nk), :]], axis=0)],
            axis=1)
        o = jnp.dot(pr, vcat, preferred_element_type=F32)
        if p == N_PAT - 1:
            c = (start - p * SEQ) // BLK
            rows = pl.ds(p * SEQ + (c % DIL4) * SEG4 + c // DIL4, BLK, stride=DIL4)
        else:
            rows = pl.ds(start, BLK)
        num_s[rows, :] = o[:, :2 * HEAD_DIM]
        den_s[rows, :] = o[:, 2 * HEAD_DIM:]
        mx_s[rows, :] = jnp.where(head0, m0, m1)

    general, first = _block_starts()
    for p, start in first:
        block(p, start, False)
    for p, start in general:
        block(p, start, True)

    for r in range(DIL4):
        for a in range(SEG4 // BLK):
            nat = pl.ds(DIL4 * BLK * a + r, BLK, stride=DIL4)
            r4 = r * SEG4 + a * BLK
            p1, p2 = pl.ds(SEQ + r4, BLK), pl.ds(2 * SEQ + r4, BLK)
            x0, x1, x2 = mx_s[nat, :], mx_s[p1, :], mx_s[p2, :]
            m_all = jnp.maximum(jnp.maximum(x0, x1), x2)
            w0, w1, w2 = jnp.exp2(x0 - m_all), jnp.exp2(x1 - m_all), jnp.exp2(x2 - m_all)
            num = w0 * num_s[nat, :] + w1 * num_s[p1, :] + w2 * num_s[p2, :]
            den = w0 * den_s[nat, :] + w1 * den_s[p1, :] + w2 * den_s[p2, :]
            xf[nat, :] = num / den
    o_ref[...] = xf[...].astype(BF16)


def _attention(zq, slopes):
    t = zq.shape[2]
    nb = t // SEQ
    col = 2 * HEAD_DIM
    qkv_spec = lambda kind: pl.BlockSpec((None, HP_PER_STEP, SEQ, col),
                                         lambda b, h, kind=kind: (kind, h, b, 0))
    f32_seq = lambda: pltpu.VMEM((SEQ, col), F32)
    f32_all = lambda: pltpu.VMEM((N_PAT * SEQ, col), F32)
    return pl.pallas_call(
        _attn_kernel,
        grid=(nb, HEADS // 2 // HP_PER_STEP),
        in_specs=[
            pl.BlockSpec((HP_PER_STEP, 2, col), lambda b, h: (h, 0, 0)),
            qkv_spec(0), qkv_spec(1), qkv_spec(2),
        ],
        out_specs=pl.BlockSpec((HP_PER_STEP, SEQ, col), lambda b, h: (h, b, 0)),
        out_shape=jax.ShapeDtypeStruct((HEADS // 2, t, col), BF16),
        scratch_shapes=[
            f32_seq(), f32_seq(),
            pltpu.VMEM((N_PAT * SEQ, col), BF16),
            pltpu.VMEM((N_PAT * SEQ, col), BF16),
            pltpu.VMEM((N_PAT * SEQ, col), BF16),
            pltpu.VMEM((2, 2 * BLK, col), BF16),
            pltpu.VMEM((N_PAT - 1, BLK, 4 * BLK), F32),
            pltpu.VMEM((N_PAT, BLK, 2 * BLK), F32),
            f32_all(), f32_all(), f32_all(),
        ],
        compiler_params=pltpu.CompilerParams(
            dimension_semantics=("arbitrary", "arbitrary"),
            vmem_limit_bytes=VMEM_LIMIT_BYTES),
        name="dilated_attn",
    )(slopes, zq, zq, zq)


def _merge_kernel(u_ref, v_ref, ga_ref, gb_ref, yb_ref, x_ref, ws_ref, bs_ref,
                  wa_ref, wb_ref, wo_ref, g_ref, o_ref, ya_ref):
    tm = u_ref.shape[0]
    ti = lax.broadcasted_iota(jnp.int32, (CHUNK, CHUNK), 0)
    si = lax.broadcasted_iota(jnp.int32, (CHUNK, CHUNK), 1)
    causal = si <= ti
    for g in range(GROUPS):
        w = jnp.where(causal, ws_ref[g], 0.0).astype(BF16)
        bcol = bs_ref[:, g:g + 1]
        cols = slice(g * GROUP_DIM, (g + 1) * GROUP_DIM)
        for c in range(tm // CHUNK):
            rows = slice(c * CHUNK, (c + 1) * CHUNK)
            mixed = jnp.dot(w, v_ref[rows, cols], preferred_element_type=F32) + bcol
            ya_ref[rows, cols] = (u_ref[rows, cols].astype(F32) * mixed).astype(BF16)
    a = jnp.dot(ya_ref[...], wa_ref[...].astype(BF16), preferred_element_type=F32)
    yb = jnp.concatenate([yb_ref[hp] for hp in range(HEADS // 2)], axis=1)
    b = jnp.dot(yb, wb_ref[...].astype(BF16), preferred_element_type=F32)
    merged = ga_ref[...].astype(F32) * a + gb_ref[...].astype(F32) * b
    y = jnp.dot(merged.astype(BF16), wo_ref[...].astype(BF16), preferred_element_type=F32)
    o_ref[...] = x_ref[...] + _rms_norm(y, g_ref[...])


def _merge(z, yb, x2, w_s, b_s_t, w_a, w_b, w_o, g, tm):
    t = x2.shape[0]
    row = lambda c: pl.BlockSpec((tm, D_MODEL), lambda i, c=c: (i, c))
    const = lambda shape, **kw: pl.BlockSpec(shape, lambda i: (0,) * len(shape), **kw)
    weight = lambda: const((D_MODEL, D_MODEL), pipeline_mode=pl.Buffered(1))
    return pl.pallas_call(
        _merge_kernel,
        grid=(t // tm,),
        in_specs=[row(0), row(1), row(2), row(3),
                  pl.BlockSpec((HEADS // 2, tm, 2 * HEAD_DIM), lambda i: (0, i, 0)), row(0),
                  const((GROUPS, CHUNK, CHUNK)), const((CHUNK, GROUPS)),
                  weight(), weight(), weight(), const((1, D_MODEL))],
        out_specs=row(0),
        out_shape=jax.ShapeDtypeStruct((t, D_MODEL), F32),
        scratch_shapes=[pltpu.VMEM((tm, D_MODEL), BF16)],
        compiler_params=pltpu.CompilerParams(
            dimension_semantics=("arbitrary",),
            vmem_limit_bytes=VMEM_LIMIT_BYTES),
        name="merge",
    )(z, z, z, z, yb, x2, w_s, b_s_t, w_a, w_b, w_o, g)


def _ffn_kernel(x_ref, gpre_ref, w1_ref, w2_ref, gpost_ref, o_ref):
    x = x_ref[...]
    h = _rms_norm(x, gpre_ref[...]).astype(BF16)
    y = None
    for j in range(D_FF // D_MODEL):
        slab = slice(j * D_MODEL, (j + 1) * D_MODEL)
        f = jnp.dot(h, w1_ref[:, slab].astype(BF16), preferred_element_type=F32)
        f = jnp.square(jnp.maximum(f, 0.0)).astype(BF16)
        part = jnp.dot(f, w2_ref[slab, :].astype(BF16), preferred_element_type=F32)
        y = part if y is None else y + part
    o_ref[...] = x + _rms_norm(y, gpost_ref[...])


def _ffn(x1, g_pre, w1, w2, g_post, tm):
    t = x1.shape[0]
    const = lambda shape, **kw: pl.BlockSpec(shape, lambda i: (0,) * len(shape), **kw)
    return pl.pallas_call(
        _ffn_kernel,
        grid=(t // tm,),
        in_specs=[
            pl.BlockSpec((tm, D_MODEL), lambda i: (i, 0)),
            const((1, D_MODEL)),
            const((D_MODEL, D_FF), pipeline_mode=pl.Buffered(1)),
            const((D_FF, D_MODEL), pipeline_mode=pl.Buffered(1)),
            const((1, D_MODEL)),
        ],
        out_specs=pl.BlockSpec((tm, D_MODEL), lambda i: (i, 0)),
        out_shape=jax.ShapeDtypeStruct((t, D_MODEL), F32),
        compiler_params=pltpu.CompilerParams(
            dimension_semantics=("arbitrary",),
            vmem_limit_bytes=VMEM_LIMIT_BYTES),
        name="ffn",
    )(x1, g_pre, w1, w2, g_post)


def kernel(x, norm_mix_pre, w_in, b_gate, ln_v_g, ln_v_b, w_s, b_s, w_a_proj, w_b_proj,
           w_out, norm_mix_post, norm_ffn_pre, w_ff1, w_ff2, norm_ffn_post):
    bsz, s, d = x.shape
    assert (s, d) == (SEQ, D_MODEL)
    depth = w_in.shape[0]
    slopes = jnp.exp2(-8.0 * jnp.arange(1, HEADS + 1, dtype=F32) / HEADS)
    slopes = jnp.broadcast_to(slopes.reshape(HEADS // 2, 2, 1), (HEADS // 2, 2, 2 * HEAD_DIM))
    x2 = x.reshape(bsz * s, d)
    for l in range(depth):
        z, zq = _inproj(x2, norm_mix_pre[l][None], w_in[l], ln_v_g[l][None],
                        ln_v_b[l][None], b_gate[l], tm=512)
        yb = _attention(zq, slopes)
        x2 = _merge(z, yb, x2, w_s[l], b_s[l].T, w_a_proj[l], w_b_proj[l], w_out[l],
                    norm_mix_post[l][None], tm=512)
        x2 = _ffn(x2, norm_ffn_pre[l][None], w_ff1[l], w_ff2[l], norm_ffn_post[l][None],
                  tm=512)
    return x2.reshape(bsz, s, d)
```

```python
import math

import jax
import jax.numpy as jnp
from jax import lax
from jax.experimental import pallas as pl
from jax.experimental.pallas import tpu as pltpu

D_MODEL = 1024
SEQ = 2048
CHUNK = 128
GROUPS = 8
GROUP_DIM = D_MODEL // GROUPS
HEADS = 16
HEAD_DIM = D_MODEL // HEADS
PATTERNS = ((128, 1), (512, 4), (2048, 16))
BLK = 128
D_FF = 4 * D_MODEL
EPS = 1e-6
N_IN_BLOCKS = 7
NEG = -1e30
LOG2E = math.log2(math.e)
Q_SCALE = LOG2E / math.sqrt(HEAD_DIM)

VMEM_LIMIT_BYTES = 56 * 1024 * 1024

F32 = jnp.float32
BF16 = jnp.bfloat16


def _rms_norm(x, g):
    return x * lax.rsqrt(jnp.mean(x * x, axis=-1, keepdims=True) + EPS) * g


def _sigmoid(x):
    return 0.5 * jnp.tanh(0.5 * x) + 0.5


def _inproj_kernel(x_ref, g_ref, w_ref, lng_ref, lnb_ref, bg_ref, z_ref, zq_ref):
    h = _rms_norm(x_ref[...], g_ref[...]).astype(BF16)

    def proj(j):
        return jnp.dot(h, w_ref[:, j * D_MODEL:(j + 1) * D_MODEL].astype(BF16),
                       preferred_element_type=F32)

    def put(c, val):
        z_ref[:, c * D_MODEL:(c + 1) * D_MODEL] = val.astype(BF16)

    def put_heads(kind, val):
        val = val.astype(BF16)
        for hp in range(HEADS // 2):
            zq_ref[kind, hp] = val[:, hp * 2 * HEAD_DIM:(hp + 1) * 2 * HEAD_DIM]

    put(0, jax.nn.gelu(proj(0)))
    v = jax.nn.gelu(proj(1))
    mu = jnp.mean(v, axis=-1, keepdims=True)
    var = jnp.mean(jnp.square(v - mu), axis=-1, keepdims=True)
    put(1, (v - mu) * lax.rsqrt(var + EPS) * lng_ref[...] + lnb_ref[...])
    put(2, _sigmoid(proj(5) + bg_ref[0:1, :]))
    put(3, _sigmoid(proj(6) + bg_ref[1:2, :]))
    put_heads(0, proj(2) * Q_SCALE)
    put_heads(1, proj(3))
    put_heads(2, proj(4))


def _inproj(x2, g, w_in, ln_g, ln_b, b_gate, tm):
    t = x2.shape[0]
    const = lambda shape, **kw: pl.BlockSpec(shape, lambda i: (0,) * len(shape), **kw)
    return pl.pallas_call(
        _inproj_kernel,
        grid=(t // tm,),
        in_specs=[
            pl.BlockSpec((tm, D_MODEL), lambda i: (i, 0)),
            const((1, D_MODEL)),
            const((D_MODEL, N_IN_BLOCKS * D_MODEL), pipeline_mode=pl.Buffered(1)),
            const((1, D_MODEL)),
            const((1, D_MODEL)),
            const((2, D_MODEL)),
        ],
        out_specs=[
            pl.BlockSpec((tm, 4 * D_MODEL), lambda i: (i, 0)),
            pl.BlockSpec((3, HEADS // 2, tm, 2 * HEAD_DIM), lambda i: (0, 0, i, 0)),
        ],
        out_shape=[
            jax.ShapeDtypeStruct((t, 4 * D_MODEL), BF16),
            jax.ShapeDtypeStruct((3, HEADS // 2, t, 2 * HEAD_DIM), BF16),
        ],
        compiler_params=pltpu.CompilerParams(
            dimension_semantics=("arbitrary",),
            vmem_limit_bytes=VMEM_LIMIT_BYTES),
        name="inproj",
    )(x2, g, w_in, ln_g, ln_b, b_gate)


N_PAT = len(PATTERNS)
DIL4, DIL16 = PATTERNS[1][1], PATTERNS[2][1]
assert PATTERNS[0][1] == 1 and DIL16 == DIL4 * DIL4
SEG4 = SEQ // DIL4
HP_PER_STEP = 2


def _block_starts():
    general, first = [], []
    for p, (_, dil) in enumerate(PATTERNS):
        seg = SEQ // dil
        for r in range(dil):
            for n in range(seg // BLK):
                (first if n == 0 else general).append((p, p * SEQ + r * seg + n * BLK))
    return general, first


def _attn_kernel(slope_ref, q_ref, k_ref, v_ref, o_ref, *scratch):
    for i in range(HP_PER_STEP):
        _attn_head_pair(slope_ref.at[i], q_ref.at[i], k_ref.at[i], v_ref.at[i], o_ref.at[i],
                        *scratch)


def _attn_head_pair(slope_ref, q_ref, k_ref, v_ref, o_ref,
                    xf, l4, q_s, kh_s, vh_s, ee_s, bias_g, bias_f, num_s, den_s, mx_s):
    lane = lax.broadcasted_iota(jnp.int32, (BLK, 2 * HEAD_DIM), 1)
    head0 = lane < HEAD_DIM

    def emit(dst_rows, val, kind):
        (q_s, kh_s, vh_s)[kind][dst_rows, :] = val.astype(BF16)

    def per_head(x):
        keep0 = lax.broadcasted_iota(jnp.int32, x.shape, 1) < HEAD_DIM
        zero = jnp.zeros_like(x)
        return jnp.concatenate([jnp.where(keep0, x, zero), jnp.where(keep0, zero, x)], axis=0)

    for kind, src in enumerate((q_ref, k_ref, v_ref)):
        xf[...] = src[...].astype(F32)
        for r in range(DIL4):
            for a in range(SEG4 // BLK):
                x = xf[pl.ds(DIL4 * BLK * a + r, BLK, stride=DIL4), :]
                dst = r * SEG4 + a * BLK
                l4[pl.ds(dst, BLK), :] = x
                emit(pl.ds(dst, BLK), x, kind)
        for c in range(DIL16):
            x = l4[pl.ds((c % DIL4) * SEG4 + c // DIL4, BLK, stride=DIL4), :]
            emit(pl.ds(SEQ + c * BLK, BLK), x, kind)

    e0 = jnp.where(head0, 1.0, 0.0).astype(BF16)
    e1 = jnp.where(head0, 0.0, 1.0).astype(BF16)
    for i in range(2):
        ee_s[0, pl.ds(i * BLK, BLK), :] = e0
        ee_s[1, pl.ds(i * BLK, BLK), :] = e1

    qi = lax.broadcasted_iota(jnp.int32, (BLK, BLK), 0)
    kj = lax.broadcasted_iota(jnp.int32, (BLK, BLK), 1)
    d_cur = (qi - kj).astype(F32)
    ok_cur = kj <= qi
    ok_prev = kj >= qi
    for p, (_, dil) in enumerate(PATTERNS):
        for e in range(2):
            sl = slope_ref[e:e + 1, :]
            sl = sl * LOG2E
            b_cur = jnp.where(ok_cur, -sl * (dil * d_cur), NEG)
            bias_f[p, :, e * BLK:(e + 1) * BLK] = b_cur
            if p < N_PAT - 1:
                b_prev = jnp.where(ok_prev, -sl * (dil * (d_cur + BLK)), NEG)
                bias_g[p, :, 2 * e * BLK:(2 * e + 1) * BLK] = b_prev
                bias_g[p, :, (2 * e + 1) * BLK:(2 * e + 2) * BLK] = b_cur

    def block(p, start, with_prev):
        nk = 2 * BLK if with_prev else BLK
        q_src, k_src, v_src = (q_ref, k_ref, v_ref) if p == 0 else (q_s, kh_s, vh_s)
        src_start = start if p == 0 else start - SEQ
        keys = pl.ds(src_start - BLK, nk) if with_prev else pl.ds(src_start, nk)
        q = q_src[pl.ds(src_start, BLK), :]
        kcat = per_head(k_src[keys, :])
        b = bias_g[p] if with_prev else bias_f[p]
        s = lax.dot_general(q, kcat, (((1,), (1,)), ((), ())),
                            preferred_element_type=F32) + b
        m0 = jnp.max(s[:, :nk], axis=-1, keepdims=True)
        m1 = jnp.max(s[:, nk:], axis=-1, keepdims=True)
        pr = jnp.concatenate([jnp.exp2(s[:, :nk] - m0), jnp.exp2(s[:, nk:] - m1)],
                             axis=1).astype(BF16)
        vcat = jnp.concatenate(
            [per_head(v_src[keys, :]),
             jnp.concatenate([ee_s[0, pl.ds(0, nk), :], ee_s[1, pl.ds(0, nk), :]], axis=0)],
            axis=1)
        o = jnp.dot(pr, vcat, preferred_element_type=F32)
        if p == N_PAT - 1:
            c = (start - p * SEQ) // BLK
            rows = pl.ds(p * SEQ + (c % DIL4) * SEG4 + c // DIL4, BLK, stride=DIL4)
        else:
            rows = pl.ds(start, BLK)
        num_s[rows, :] = o[:, :2 * HEAD_DIM]
        den_s[rows, :] = o[:, 2 * HEAD_DIM:]
        mx_s[rows, :] = jnp.where(head0, m0, m1)

    general, first = _block_starts()
    for p, start in first:
        block(p, start, False)
    for p, start in general:
        block(p, start, True)

    for r in range(DIL4):
        for a in range(SEG4 // BLK):
            nat = pl.ds(DIL4 * BLK * a + r, BLK, stride=DIL4)
            r4 = r * SEG4 + a * BLK
            p1, p2 = pl.ds(SEQ + r4, BLK), pl.ds(2 * SEQ + r4, BLK)
            x0, x1, x2 = mx_s[nat, :], mx_s[p1, :], mx_s[p2, :]
            m_all = jnp.maximum(jnp.maximum(x0, x1), x2)
            w0, w1, w2 = jnp.exp2(x0 - m_all), jnp.exp2(x1 - m_all), jnp.exp2(x2 - m_all)
            num = w0 * num_s[nat, :] + w1 * num_s[p1, :] + w2 * num_s[p2, :]
            den = w0 * den_s[nat, :] + w1 * den_s[p1, :] + w2 * den_s[p2, :]
            xf[nat, :] = num / den
    o_ref[...] = xf[...].astype(BF16)


def _attention(zq, slopes):
    t = zq.shape[2]
    nb = t // SEQ
    col = 2 * HEAD_DIM
    qkv_spec = lambda kind: pl.BlockSpec((None, HP_PER_STEP, SEQ, col),
                                         lambda b, h, kind=kind: (kind, h, b, 0))
    f32_seq = lambda: pltpu.VMEM((SEQ, col), F32)
    f32_all = lambda: pltpu.VMEM((N_PAT * SEQ, col), F32)
    return pl.pallas_call(
        _attn_kernel,
        grid=(nb, HEADS // 2 // HP_PER_STEP),
        in_specs=[
            pl.BlockSpec((HP_PER_STEP, 2, col), lambda b, h: (h, 0, 0)),
            qkv_spec(0), qkv_spec(1), qkv_spec(2),
        ],
        out_specs=pl.BlockSpec((HP_PER_STEP, SEQ, col), lambda b, h: (h, b, 0)),
        out_shape=jax.ShapeDtypeStruct((HEADS // 2, t, col), BF16),
        scratch_shapes=[
            f32_seq(), f32_seq(),
            pltpu.VMEM(((N_PAT - 1) * SEQ, col), BF16),
            pltpu.VMEM(((N_PAT - 1) * SEQ, col), BF16),
            pltpu.VMEM(((N_PAT - 1) * SEQ, col), BF16),
            pltpu.VMEM((2, 2 * BLK, col), BF16),
            pltpu.VMEM((N_PAT - 1, BLK, 4 * BLK), F32),
            pltpu.VMEM((N_PAT, BLK, 2 * BLK), F32),
            f32_all(), f32_all(), f32_all(),
        ],
        compiler_params=pltpu.CompilerParams(
            dimension_semantics=("arbitrary", "arbitrary"),
            vmem_limit_bytes=VMEM_LIMIT_BYTES),
        name="dilated_attn",
    )(slopes, zq, zq, zq)


def _merge_kernel(u_ref, v_ref, ga_ref, gb_ref, yb_ref, x_ref, ws_ref, bs_ref,
                  wa_ref, wb_ref, wo_ref, g_ref, o_ref, ya_ref):
    tm = u_ref.shape[0]
    ti = lax.broadcasted_iota(jnp.int32, (CHUNK, CHUNK), 0)
    si = lax.broadcasted_iota(jnp.int32, (CHUNK, CHUNK), 1)
    causal = si <= ti
    for g in range(GROUPS):
        w = jnp.where(causal, ws_ref[g], 0.0).astype(BF16)
        bcol = bs_ref[:, g:g + 1]
        cols = slice(g * GROUP_DIM, (g + 1) * GROUP_DIM)
        for c in range(tm // CHUNK):
            rows = slice(c * CHUNK, (c + 1) * CHUNK)
            mixed = jnp.dot(w, v_ref[rows, cols], preferred_element_type=F32) + bcol
            ya_ref[rows, cols] = (u_ref[rows, cols].astype(F32) * mixed).astype(BF16)
    a = jnp.dot(ya_ref[...], wa_ref[...].astype(BF16), preferred_element_type=F32)
    yb = jnp.concatenate([yb_ref[hp] for hp in range(HEADS // 2)], axis=1)
    b = jnp.dot(yb, wb_ref[...].astype(BF16), preferred_element_type=F32)
    merged = ga_ref[...].astype(F32) * a + gb_ref[...].astype(F32) * b
    y = jnp.dot(merged.astype(BF16), wo_ref[...].astype(BF16), preferred_element_type=F32)
    o_ref[...] = x_ref[...] + _rms_norm(y, g_ref[...])


def _merge(z, yb, x2, w_s, b_s_t, w_a, w_b, w_o, g, tm):
    t = x2.shape[0]
    row = lambda c: pl.BlockSpec((tm, D_MODEL), lambda i, c=c: (i, c))
    const = lambda shape, **kw: pl.BlockSpec(shape, lambda i: (0,) * len(shape), **kw)
    weight = lambda: const((D_MODEL, D_MODEL), pipeline_mode=pl.Buffered(1))
    return pl.pallas_call(
        _merge_kernel,
        grid=(t // tm,),
        in_specs=[row(0), row(1), row(2), row(3),
                  pl.BlockSpec((HEADS // 2, tm, 2 * HEAD_DIM), lambda i: (0, i, 0)), row(0),
                  const((GROUPS, CHUNK, CHUNK)), const((CHUNK, GROUPS)),
                  weight(), weight(), weight(), const((1, D_MODEL))],
        out_specs=row(0),
        out_shape=jax.ShapeDtypeStruct((t, D_MODEL), F32),
        scratch_shapes=[pltpu.VMEM((tm, D_MODEL), BF16)],
        compiler_params=pltpu.CompilerParams(
            dimension_semantics=("arbitrary",),
            vmem_limit_bytes=VMEM_LIMIT_BYTES),
        name="merge",
    )(z, z, z, z, yb, x2, w_s, b_s_t, w_a, w_b, w_o, g)


def _ffn_kernel(x_ref, gpre_ref, w1_ref, w2_ref, gpost_ref, o_ref):
    x = x_ref[...]
    h = _rms_norm(x, gpre_ref[...]).astype(BF16)
    y = None
    for j in range(D_FF // D_MODEL):
        slab = slice(j * D_MODEL, (j + 1) * D_MODEL)
        f = jnp.dot(h, w1_ref[:, slab].astype(BF16), preferred_element_type=F32)
        f = jnp.square(jnp.maximum(f, 0.0)).astype(BF16)
        part = jnp.dot(f, w2_ref[slab, :].astype(BF16), preferred_element_type=F32)
        y = part if y is None else y + part
    o_ref[...] = x + _rms_norm(y, gpost_ref[...])


def _ffn(x1, g_pre, w1, w2, g_post, tm):
    t = x1.shape[0]
    const = lambda shape, **kw: pl.BlockSpec(shape, lambda i: (0,) * len(shape), **kw)
    return pl.pallas_call(
        _ffn_kernel,
        grid=(t // tm,),
        in_specs=[
            pl.BlockSpec((tm, D_MODEL), lambda i: (i, 0)),
            const((1, D_MODEL)),
            const((D_MODEL, D_FF), pipeline_mode=pl.Buffered(1)),
            const((D_FF, D_MODEL), pipeline_mode=pl.Buffered(1)),
            const((1, D_MODEL)),
        ],
        out_specs=pl.BlockSpec((tm, D_MODEL), lambda i: (i, 0)),
        out_shape=jax.ShapeDtypeStruct((t, D_MODEL), F32),
        compiler_params=pltpu.CompilerParams(
            dimension_semantics=("arbitrary",),
            vmem_limit_bytes=VMEM_LIMIT_BYTES),
        name="ffn",
    )(x1, g_pre, w1, w2, g_post)


def kernel(x, norm_mix_pre, w_in, b_gate, ln_v_g, ln_v_b, w_s, b_s, w_a_proj, w_b_proj,
           w_out, norm_mix_post, norm_ffn_pre, w_ff1, w_ff2, norm_ffn_post):
    bsz, s, d = x.shape
    assert (s, d) == (SEQ, D_MODEL)
    depth = w_in.shape[0]
    slopes = jnp.exp2(-8.0 * jnp.arange(1, HEADS + 1, dtype=F32) / HEADS)
    slopes = jnp.broadcast_to(slopes.reshape(HEADS // 2, 2, 1), (HEADS // 2, 2, 2 * HEAD_DIM))
    x2 = x.reshape(bsz * s, d)
    for l in range(depth):
        z, zq = _inproj(x2, norm_mix_pre[l][None], w_in[l], ln_v_g[l][None],
                        ln_v_b[l][None], b_gate[l], tm=512)
        yb = _attention(zq, slopes)
        x2 = _merge(z, yb, x2, w_s[l], b_s[l].T, w_a_proj[l], w_b_proj[l], w_out[l],
                    norm_mix_post[l][None], tm=512)
        x2 = _ffn(x2, norm_ffn_pre[l][None], w_ff1[l], w_ff2[l], norm_ffn_post[l][None],
                  tm=512)
    return x2.reshape(bsz, s, d)
```

```python
import math

import jax
import jax.numpy as jnp
from jax import lax
from jax.experimental import pallas as pl
from jax.experimental.pallas import tpu as pltpu

D_MODEL = 1024
SEQ = 2048
CHUNK = 128
GROUPS = 8
GROUP_DIM = D_MODEL // GROUPS
HEADS = 16
HEAD_DIM = D_MODEL // HEADS
PATTERNS = ((128, 1), (512, 4), (2048, 16))
BLK = 128
D_FF = 4 * D_MODEL
EPS = 1e-6
N_IN_BLOCKS = 7
NEG = -1e30
LOG2E = math.log2(math.e)
Q_SCALE = LOG2E / math.sqrt(HEAD_DIM)

VMEM_LIMIT_BYTES = 56 * 1024 * 1024

F32 = jnp.float32
BF16 = jnp.bfloat16


def _rms_norm(x, g):
    return x * lax.rsqrt(jnp.mean(x * x, axis=-1, keepdims=True) + EPS) * g


def _sigmoid(x):
    return 0.5 * jnp.tanh(0.5 * x) + 0.5


def _inproj_kernel(x_ref, g_ref, w_ref, lng_ref, lnb_ref, bg_ref, z_ref, zq_ref):
    h = _rms_norm(x_ref[...], g_ref[...]).astype(BF16)

    def proj(j):
        return jnp.dot(h, w_ref[:, j * D_MODEL:(j + 1) * D_MODEL].astype(BF16),
                       preferred_element_type=F32)

    def put(c, val):
        z_ref[:, c * D_MODEL:(c + 1) * D_MODEL] = val.astype(BF16)

    def put_heads(kind, val):
        val = val.astype(BF16)
        for hp in range(HEADS // 2):
            zq_ref[kind, hp] = val[:, hp * 2 * HEAD_DIM:(hp + 1) * 2 * HEAD_DIM]

    put(0, jax.nn.gelu(proj(0)))
    v = jax.nn.gelu(proj(1))
    mu = jnp.mean(v, axis=-1, keepdims=True)
    var = jnp.mean(jnp.square(v - mu), axis=-1, keepdims=True)
    put(1, (v - mu) * lax.rsqrt(var + EPS) * lng_ref[...] + lnb_ref[...])
    put(2, _sigmoid(proj(5) + bg_ref[0:1, :]))
    put(3, _sigmoid(proj(6) + bg_ref[1:2, :]))
    put_heads(0, proj(2) * Q_SCALE)
    put_heads(1, proj(3))
    put_heads(2, proj(4))


def _inproj(x2, g, w_in, ln_g, ln_b, b_gate, tm):
    t = x2.shape[0]
    const = lambda shape, **kw: pl.BlockSpec(shape, lambda i: (0,) * len(shape), **kw)
    return pl.pallas_call(
        _inproj_kernel,
        grid=(t // tm,),
        in_specs=[
            pl.BlockSpec((tm, D_MODEL), lambda i: (i, 0)),
            const((1, D_MODEL)),
            const((D_MODEL, N_IN_BLOCKS * D_MODEL), pipeline_mode=pl.Buffered(1)),
            const((1, D_MODEL)),
            const((1, D_MODEL)),
            const((2, D_MODEL)),
        ],
        out_specs=[
            pl.BlockSpec((tm, 4 * D_MODEL), lambda i: (i, 0)),
            pl.BlockSpec((3, HEADS // 2, tm, 2 * HEAD_DIM), lambda i: (0, 0, i, 0)),
        ],
        out_shape=[
            jax.ShapeDtypeStruct((t, 4 * D_MODEL), BF16),
            jax.ShapeDtypeStruct((3, HEADS // 2, t, 2 * HEAD_DIM), BF16),
        ],
        compiler_params=pltpu.CompilerParams(
            dimension_semantics=("arbitrary",),
            vmem_limit_bytes=VMEM_LIMIT_BYTES),
        name="inproj",
    )(x2, g, w_in, ln_g, ln_b, b_gate)


N_PAT = len(PATTERNS)
DIL4, DIL16 = PATTERNS[1][1], PATTERNS[2][1]
assert PATTERNS[0][1] == 1 and DIL16 == DIL4 * DIL4
SEG4 = SEQ // DIL4
HP_PER_STEP = 2


def _block_starts():
    general, first = [], []
    for p, (_, dil) in enumerate(PATTERNS):
        seg = SEQ // dil
        for r in range(dil):
            for n in range(seg // BLK):
                (first if n == 0 else general).append((p, p * SEQ + r * seg + n * BLK))
    return general, first


def _attn_kernel(slope_ref, q_ref, k_ref, v_ref, o_ref, *scratch):
    for i in range(HP_PER_STEP):
        _attn_head_pair(slope_ref.at[i], q_ref.at[i], k_ref.at[i], v_ref.at[i], o_ref.at[i],
                        *scratch)


def _attn_head_pair(slope_ref, q_ref, k_ref, v_ref, o_ref,
                    xq, xk, xv, lq, lk, lv, q_s, kh_s, vh_s, ee_s, bias_g, bias_f,
                    num_s, den_s, mx_s):
    lane = lax.broadcasted_iota(jnp.int32, (BLK, 2 * HEAD_DIM), 1)
    head0 = lane < HEAD_DIM

    def emit(dst_rows, val, kind):
        (q_s, kh_s, vh_s)[kind][dst_rows, :] = val.astype(BF16)

    def per_head(x):
        keep0 = lax.broadcasted_iota(jnp.int32, x.shape, 1) < HEAD_DIM
        zero = jnp.zeros_like(x)
        return jnp.concatenate([jnp.where(keep0, x, zero), jnp.where(keep0, zero, x)], axis=0)

    for kind, (src, xf, l4) in enumerate(zip((q_ref, k_ref, v_ref), (xq, xk, xv),
                                             (lq, lk, lv))):
        xf[...] = src[...].astype(F32)
        for r in range(DIL4):
            for a in range(SEG4 // BLK):
                x = xf[pl.ds(DIL4 * BLK * a + r, BLK, stride=DIL4), :]
                dst = r * SEG4 + a * BLK
                l4[pl.ds(dst, BLK), :] = x
                emit(pl.ds(dst, BLK), x, kind)
        for c in range(DIL16):
            x = l4[pl.ds((c % DIL4) * SEG4 + c // DIL4, BLK, stride=DIL4), :]
            emit(pl.ds(SEQ + c * BLK, BLK), x, kind)

    e0 = jnp.where(head0, 1.0, 0.0).astype(BF16)
    e1 = jnp.where(head0, 0.0, 1.0).astype(BF16)
    for i in range(2):
        ee_s[0, pl.ds(i * BLK, BLK), :] = e0
        ee_s[1, pl.ds(i * BLK, BLK), :] = e1

    qi = lax.broadcasted_iota(jnp.int32, (BLK, BLK), 0)
    kj = lax.broadcasted_iota(jnp.int32, (BLK, BLK), 1)
    d_cur = (qi - kj).astype(F32)
    ok_cur = kj <= qi
    ok_prev = kj >= qi
    for p, (_, dil) in enumerate(PATTERNS):
        for e in range(2):
            sl = slope_ref[e:e + 1, :]
            sl = sl * LOG2E
            b_cur = jnp.where(ok_cur, -sl * (dil * d_cur), NEG)
            bias_f[p, :, e * BLK:(e + 1) * BLK] = b_cur
            if p < N_PAT - 1:
                b_prev = jnp.where(ok_prev, -sl * (dil * (d_cur + BLK)), NEG)
                bias_g[p, :, 2 * e * BLK:(2 * e + 1) * BLK] = b_prev
                bias_g[p, :, (2 * e + 1) * BLK:(2 * e + 2) * BLK] = b_cur

    def block(p, start, with_prev):
        nk = 2 * BLK if with_prev else BLK
        q_src, k_src, v_src = (q_ref, k_ref, v_ref) if p == 0 else (q_s, kh_s, vh_s)
        src_start = start if p == 0 else start - SEQ
        keys = pl.ds(src_start - BLK, nk) if with_prev else pl.ds(src_start, nk)
        q = q_src[pl.ds(src_start, BLK), :]
        kcat = per_head(k_src[keys, :])
        b = bias_g[p] if with_prev else bias_f[p]
        s = lax.dot_general(q, kcat, (((1,), (1,)), ((), ())),
                            preferred_element_type=F32) + b
        m0 = jnp.max(s[:, :nk], axis=-1, keepdims=True)
        m1 = jnp.max(s[:, nk:], axis=-1, keepdims=True)
        pr = jnp.concatenate([jnp.exp2(s[:, :nk] - m0), jnp.exp2(s[:, nk:] - m1)],
                             axis=1).astype(BF16)
        vcat = jnp.concatenate(
            [per_head(v_src[keys, :]),
             jnp.concatenate([ee_s[0, pl.ds(0, nk), :], ee_s[1, pl.ds(0, nk), :]], axis=0)],
            axis=1)
        o = jnp.dot(pr, vcat, preferred_element_type=F32)
        if p == N_PAT - 1:
            c = (start - p * SEQ) // BLK
            rows = pl.ds(p * SEQ + (c % DIL4) * SEG4 + c // DIL4, BLK, stride=DIL4)
        else:
            rows = pl.ds(start, BLK)
        num_s[rows, :] = o[:, :2 * HEAD_DIM]
        den_s[rows, :] = o[:, 2 * HEAD_DIM:]
        mx_s[rows, :] = jnp.where(head0, m0, m1)

    general, first = _block_starts()
    for p, start in first:
        block(p, start, False)
    for p, start in general:
        block(p, start, True)

    for r in range(DIL4):
        for a in range(SEG4 // BLK):
            nat = pl.ds(DIL4 * BLK * a + r, BLK, stride=DIL4)
            r4 = r * SEG4 + a * BLK
            p1, p2 = pl.ds(SEQ + r4, BLK), pl.ds(2 * SEQ + r4, BLK)
            x0, x1, x2 = mx_s[nat, :], mx_s[p1, :], mx_s[p2, :]
            m_all = jnp.maximum(jnp.maximum(x0, x1), x2)
            w0, w1, w2 = jnp.exp2(x0 - m_all), jnp.exp2(x1 - m_all), jnp.exp2(x2 - m_all)
            num = w0 * num_s[nat, :] + w1 * num_s[p1, :] + w2 * num_s[p2, :]
            den = w0 * den_s[nat, :] + w1 * den_s[p1, :] + w2 * den_s[p2, :]
            xq[nat, :] = num / den
    o_ref[...] = xq[...].astype(BF16)


def _attention(zq, slopes):
    t = zq.shape[2]
    nb = t // SEQ
    col = 2 * HEAD_DIM
    qkv_spec = lambda kind: pl.BlockSpec((None, HP_PER_STEP, SEQ, col),
                                         lambda b, h, kind=kind: (kind, h, b, 0))
    f32_seq = lambda: pltpu.VMEM((SEQ, col), F32)
    f32_all = lambda: pltpu.VMEM((N_PAT * SEQ, col), F32)
    return pl.pallas_call(
        _attn_kernel,
        grid=(nb, HEADS // 2 // HP_PER_STEP),
        in_specs=[
            pl.BlockSpec((HP_PER_STEP, 2, col), lambda b, h: (h, 0, 0)),
            qkv_spec(0), qkv_spec(1), qkv_spec(2),
        ],
        out_specs=pl.BlockSpec((HP_PER_STEP, SEQ, col), lambda b, h: (h, b, 0)),
        out_shape=jax.ShapeDtypeStruct((HEADS // 2, t, col), BF16),
        scratch_shapes=[
            f32_seq(), f32_seq(), f32_seq(),
            f32_seq(), f32_seq(), f32_seq(),
            pltpu.VMEM(((N_PAT - 1) * SEQ, col), BF16),
            pltpu.VMEM(((N_PAT - 1) * SEQ, col), BF16),
            pltpu.VMEM(((N_PAT - 1) * SEQ, col), BF16),
            pltpu.VMEM((2, 2 * BLK, col), BF16),
            pltpu.VMEM((N_PAT - 1, BLK, 4 * BLK), F32),
            pltpu.VMEM((N_PAT, BLK, 2 * BLK), F32),
            f32_all(), f32_all(), f32_all(),
        ],
        compiler_params=pltpu.CompilerParams(
            dimension_semantics=("arbitrary", "arbitrary"),
            vmem_limit_bytes=VMEM_LIMIT_BYTES),
        name="dilated_attn",
    )(slopes, zq, zq, zq)


def _merge_kernel(u_ref, v_ref, ga_ref, gb_ref, yb_ref, x_ref, ws_ref, bs_ref,
                  wa_ref, wb_ref, wo_ref, g_ref, o_ref, ya_ref):
    tm = u_ref.shape[0]
    ti = lax.broadcasted_iota(jnp.int32, (CHUNK, CHUNK), 0)
    si = lax.broadcasted_iota(jnp.int32, (CHUNK, CHUNK), 1)
    causal = si <= ti
    for g in range(GROUPS):
        w = jnp.where(causal, ws_ref[g], 0.0).astype(BF16)
        bcol = bs_ref[:, g:g + 1]
        cols = slice(g * GROUP_DIM, (g + 1) * GROUP_DIM)
        for c in range(tm // CHUNK):
            rows = slice(c * CHUNK, (c + 1) * CHUNK)
            mixed = jnp.dot(w, v_ref[rows, cols], preferred_element_type=F32) + bcol
            ya_ref[rows, cols] = (u_ref[rows, cols].astype(F32) * mixed).astype(BF16)
    a = jnp.dot(ya_ref[...], wa_ref[...].astype(BF16), preferred_element_type=F32)
    yb = jnp.concatenate([yb_ref[hp] for hp in range(HEADS // 2)], axis=1)
    b = jnp.dot(yb, wb_ref[...].astype(BF16), preferred_element_type=F32)
    merged = ga_ref[...].astype(F32) * a + gb_ref[...].astype(F32) * b
    y = jnp.dot(merged.astype(BF16), wo_ref[...].astype(BF16), preferred_element_type=F32)
    o_ref[...] = x_ref[...] + _rms_norm(y, g_ref[...])


def _merge(z, yb, x2, w_s, b_s_t, w_a, w_b, w_o, g, tm):
    t = x2.shape[0]
    row = lambda c: pl.BlockSpec((tm, D_MODEL), lambda i, c=c: (i, c))
    const = lambda shape, **kw: pl.BlockSpec(shape, lambda i: (0,) * len(shape), **kw)
    weight = lambda: const((D_MODEL, D_MODEL), pipeline_mode=pl.Buffered(1))
    return pl.pallas_call(
        _merge_kernel,
        grid=(t // tm,),
        in_specs=[row(0), row(1), row(2), row(3),
                  pl.BlockSpec((HEADS // 2, tm, 2 * HEAD_DIM), lambda i: (0, i, 0)), row(0),
                  const((GROUPS, CHUNK, CHUNK)), const((CHUNK, GROUPS)),
                  weight(), weight(), weight(), const((1, D_MODEL))],
        out_specs=row(0),
        out_shape=jax.ShapeDtypeStruct((t, D_MODEL), F32),
        scratch_shapes=[pltpu.VMEM((tm, D_MODEL), BF16)],
        compiler_params=pltpu.CompilerParams(
            dimension_semantics=("arbitrary",),
            vmem_limit_bytes=VMEM_LIMIT_BYTES),
        name="merge",
    )(z, z, z, z, yb, x2, w_s, b_s_t, w_a, w_b, w_o, g)


def _ffn_kernel(x_ref, gpre_ref, w1_ref, w2_ref, gpost_ref, o_ref):
    x = x_ref[...]
    h = _rms_norm(x, gpre_ref[...]).astype(BF16)
    y = None
    for j in range(D_FF // D_MODEL):
        slab = slice(j * D_MODEL, (j + 1) * D_MODEL)
        f = jnp.dot(h, w1_ref[:, slab].astype(BF16), preferred_element_type=F32)
        f = jnp.square(jnp.maximum(f, 0.0)).astype(BF16)
        part = jnp.dot(f, w2_ref[slab, :].astype(BF16), preferred_element_type=F32)
        y = part if y is None else y + part
    o_ref[...] = x + _rms_norm(y, gpost_ref[...])


def _ffn(x1, g_pre, w1, w2, g_post, tm):
    t = x1.shape[0]
    const = lambda shape, **kw: pl.BlockSpec(shape, lambda i: (0,) * len(shape), **kw)
    return pl.pallas_call(
        _ffn_kernel,
        grid=(t // tm,),
        in_specs=[
            pl.BlockSpec((tm, D_MODEL), lambda i: (i, 0)),
            const((1, D_MODEL)),
            const((D_MODEL, D_FF), pipeline_mode=pl.Buffered(1)),
            const((D_FF, D_MODEL), pipeline_mode=pl.Buffered(1)),
            const((1, D_MODEL)),
        ],
        out_specs=pl.BlockSpec((tm, D_MODEL), lambda i: (i, 0)),
        out_shape=jax.ShapeDtypeStruct((t, D_MODEL), F32),
        compiler_params=pltpu.CompilerParams(
            dimension_semantics=("arbitrary",),
            vmem_limit_bytes=VMEM_LIMIT_BYTES),
        name="ffn",
    )(x1, g_pre, w1, w2, g_post)


def kernel(x, norm_mix_pre, w_in, b_gate, ln_v_g, ln_v_b, w_s, b_s, w_a_proj, w_b_proj,
           w_out, norm_mix_post, norm_ffn_pre, w_ff1, w_ff2, norm_ffn_post):
    bsz, s, d = x.shape
    assert (s, d) == (SEQ, D_MODEL)
    depth = w_in.shape[0]
    slopes = jnp.exp2(-8.0 * jnp.arange(1, HEADS + 1, dtype=F32) / HEADS)
    slopes = jnp.broadcast_to(slopes.reshape(HEADS // 2, 2, 1), (HEADS // 2, 2, 2 * HEAD_DIM))
    x2 = x.reshape(bsz * s, d)
    for l in range(depth):
        z, zq = _inproj(x2, norm_mix_pre[l][None], w_in[l], ln_v_g[l][None],
                        ln_v_b[l][None], b_gate[l], tm=512)
        yb = _attention(zq, slopes)
        x2 = _merge(z, yb, x2, w_s[l], b_s[l].T, w_a_proj[l], w_b_proj[l], w_out[l],
                    norm_mix_post[l][None], tm=512)
        x2 = _ffn(x2, norm_ffn_pre[l][None], w_ff1[l], w_ff2[l], norm_ffn_post[l][None],
                  tm=512)
    return x2.reshape(bsz, s, d)
```

```python
import math

import jax
import jax.numpy as jnp
from jax import lax
from jax.experimental import pallas as pl
from jax.experimental.pallas import tpu as pltpu

D_MODEL = 1024
SEQ = 2048
CHUNK = 128
GROUPS = 8
GROUP_DIM = D_MODEL // GROUPS
HEADS = 16
HEAD_DIM = D_MODEL // HEADS
PATTERNS = ((128, 1), (512, 4), (2048, 16))
BLK = 128
D_FF = 4 * D_MODEL
EPS = 1e-6
N_IN_BLOCKS = 7
NEG = -1e30
LOG2E = math.log2(math.e)
Q_SCALE = LOG2E / math.sqrt(HEAD_DIM)

VMEM_LIMIT_BYTES = 56 * 1024 * 1024

F32 = jnp.float32
BF16 = jnp.bfloat16


def _rms_norm(x, g):
    return x * lax.rsqrt(jnp.mean(x * x, axis=-1, keepdims=True) + EPS) * g


def _sigmoid(x):
    return 0.5 * jnp.tanh(0.5 * x) + 0.5


def _inproj_kernel(x_ref, g_ref, w_ref, lng_ref, lnb_ref, bg_ref, z_ref, zq_ref):
    h = _rms_norm(x_ref[...], g_ref[...]).astype(BF16)

    def proj(j):
        return jnp.dot(h, w_ref[:, j * D_MODEL:(j + 1) * D_MODEL].astype(BF16),
                       preferred_element_type=F32)

    def put(c, val):
        z_ref[:, c * D_MODEL:(c + 1) * D_MODEL] = val.astype(BF16)

    def put_heads(kind, val):
        val = val.astype(BF16)
        for hp in range(HEADS // 2):
            zq_ref[kind, hp] = val[:, hp * 2 * HEAD_DIM:(hp + 1) * 2 * HEAD_DIM]

    put(0, jax.nn.gelu(proj(0)))
    v = jax.nn.gelu(proj(1))
    mu = jnp.mean(v, axis=-1, keepdims=True)
    var = jnp.mean(jnp.square(v - mu), axis=-1, keepdims=True)
    put(1, (v - mu) * lax.rsqrt(var + EPS) * lng_ref[...] + lnb_ref[...])
    put(2, _sigmoid(proj(5) + bg_ref[0:1, :]))
    put(3, _sigmoid(proj(6) + bg_ref[1:2, :]))
    put_heads(0, proj(2) * Q_SCALE)
    put_heads(1, proj(3))
    put_heads(2, proj(4))


def _inproj(x2, g, w_in, ln_g, ln_b, b_gate, tm):
    t = x2.shape[0]
    const = lambda shape, **kw: pl.BlockSpec(shape, lambda i: (0,) * len(shape), **kw)
    return pl.pallas_call(
        _inproj_kernel,
        grid=(t // tm,),
        in_specs=[
            pl.BlockSpec((tm, D_MODEL), lambda i: (i, 0)),
            const((1, D_MODEL)),
            const((D_MODEL, N_IN_BLOCKS * D_MODEL), pipeline_mode=pl.Buffered(1)),
            const((1, D_MODEL)),
            const((1, D_MODEL)),
            const((2, D_MODEL)),
        ],
        out_specs=[
            pl.BlockSpec((tm, 4 * D_MODEL), lambda i: (i, 0)),
            pl.BlockSpec((3, HEADS // 2, tm, 2 * HEAD_DIM), lambda i: (0, 0, i, 0)),
        ],
        out_shape=[
            jax.ShapeDtypeStruct((t, 4 * D_MODEL), BF16),
            jax.ShapeDtypeStruct((3, HEADS // 2, t, 2 * HEAD_DIM), BF16),
        ],
        compiler_params=pltpu.CompilerParams(
            dimension_semantics=("arbitrary",),
            vmem_limit_bytes=VMEM_LIMIT_BYTES),
        name="inproj",
    )(x2, g, w_in, ln_g, ln_b, b_gate)


N_PAT = len(PATTERNS)
DIL4, DIL16 = PATTERNS[1][1], PATTERNS[2][1]
assert PATTERNS[0][1] == 1 and DIL16 == DIL4 * DIL4
SEG4 = SEQ // DIL4
HP_PER_STEP = 2


def _block_starts():
    general, first = [], []
    for p, (_, dil) in enumerate(PATTERNS):
        seg = SEQ // dil
        for r in range(dil):
            for n in range(seg // BLK):
                (first if n == 0 else general).append((p, p * SEQ + r * seg + n * BLK))
    return general, first


def _attn_kernel(slope_ref, q_ref, k_ref, v_ref, o_ref, *scratch):
    n = len(scratch) // HP_PER_STEP
    for i in range(HP_PER_STEP):
        _attn_head_pair(slope_ref.at[i], q_ref.at[i], k_ref.at[i], v_ref.at[i], o_ref.at[i],
                        *scratch[i * n:(i + 1) * n])


def _attn_head_pair(slope_ref, q_ref, k_ref, v_ref, o_ref,
                    xq, xk, xv, lq, lk, lv, q_s, kh_s, vh_s, ee_s, bias_g, bias_f,
                    num_s, den_s, mx_s):
    lane = lax.broadcasted_iota(jnp.int32, (BLK, 2 * HEAD_DIM), 1)
    head0 = lane < HEAD_DIM

    def emit(dst_rows, val, kind):
        (q_s, kh_s, vh_s)[kind][dst_rows, :] = val.astype(BF16)

    def per_head(x):
        keep0 = lax.broadcasted_iota(jnp.int32, x.shape, 1) < HEAD_DIM
        zero = jnp.zeros_like(x)
        return jnp.concatenate([jnp.where(keep0, x, zero), jnp.where(keep0, zero, x)], axis=0)

    for kind, (src, xf, l4) in enumerate(zip((q_ref, k_ref, v_ref), (xq, xk, xv),
                                             (lq, lk, lv))):
        xf[...] = src[...].astype(F32)
        for r in range(DIL4):
            for a in range(SEG4 // BLK):
                x = xf[pl.ds(DIL4 * BLK * a + r, BLK, stride=DIL4), :]
                dst = r * SEG4 + a * BLK
                l4[pl.ds(dst, BLK), :] = x
                emit(pl.ds(dst, BLK), x, kind)
        for c in range(DIL16):
            x = l4[pl.ds((c % DIL4) * SEG4 + c // DIL4, BLK, stride=DIL4), :]
            emit(pl.ds(SEQ + c * BLK, BLK), x, kind)

    e0 = jnp.where(head0, 1.0, 0.0).astype(BF16)
    e1 = jnp.where(head0, 0.0, 1.0).astype(BF16)
    for i in range(2):
        ee_s[0, pl.ds(i * BLK, BLK), :] = e0
        ee_s[1, pl.ds(i * BLK, BLK), :] = e1

    qi = lax.broadcasted_iota(jnp.int32, (BLK, BLK), 0)
    kj = lax.broadcasted_iota(jnp.int32, (BLK, BLK), 1)
    d_cur = (qi - kj).astype(F32)
    ok_cur = kj <= qi
    ok_prev = kj >= qi
    for p, (_, dil) in enumerate(PATTERNS):
        for e in range(2):
            sl = slope_ref[e:e + 1, :]
            sl = sl * LOG2E
            b_cur = jnp.where(ok_cur, -sl * (dil * d_cur), NEG)
            bias_f[p, :, e * BLK:(e + 1) * BLK] = b_cur
            if p < N_PAT - 1:
                b_prev = jnp.where(ok_prev, -sl * (dil * (d_cur + BLK)), NEG)
                bias_g[p, :, 2 * e * BLK:(2 * e + 1) * BLK] = b_prev
                bias_g[p, :, (2 * e + 1) * BLK:(2 * e + 2) * BLK] = b_cur

    def block(p, start, with_prev):
        nk = 2 * BLK if with_prev else BLK
        q_src, k_src, v_src = (q_ref, k_ref, v_ref) if p == 0 else (q_s, kh_s, vh_s)
        src_start = start if p == 0 else start - SEQ
        keys = pl.ds(src_start - BLK, nk) if with_prev else pl.ds(src_start, nk)
        q = q_src[pl.ds(src_start, BLK), :]
        kcat = per_head(k_src[keys, :])
        b = bias_g[p] if with_prev else bias_f[p]
        s = lax.dot_general(q, kcat, (((1,), (1,)), ((), ())),
                            preferred_element_type=F32) + b
        m0 = jnp.max(s[:, :nk], axis=-1, keepdims=True)
        m1 = jnp.max(s[:, nk:], axis=-1, keepdims=True)
        pr = jnp.concatenate([jnp.exp2(s[:, :nk] - m0), jnp.exp2(s[:, nk:] - m1)],
                             axis=1).astype(BF16)
        vcat = jnp.concatenate(
            [per_head(v_src[keys, :]),
             jnp.concatenate([ee_s[0, pl.ds(0, nk), :], ee_s[1, pl.ds(0, nk), :]], axis=0)],
            axis=1)
        o = jnp.dot(pr, vcat, preferred_element_type=F32)
        if p == N_PAT - 1:
            c = (start - p * SEQ) // BLK
            rows = pl.ds(p * SEQ + (c % DIL4) * SEG4 + c // DIL4, BLK, stride=DIL4)
        else:
            rows = pl.ds(start, BLK)
        num_s[rows, :] = o[:, :2 * HEAD_DIM]
        den_s[rows, :] = o[:, 2 * HEAD_DIM:]
        mx_s[rows, :] = jnp.where(head0, m0, m1)

    general, first = _block_starts()
    for p, start in first:
        block(p, start, False)
    for p, start in general:
        block(p, start, True)

    for r in range(DIL4):
        for a in range(SEG4 // BLK):
            nat = pl.ds(DIL4 * BLK * a + r, BLK, stride=DIL4)
            r4 = r * SEG4 + a * BLK
            p1, p2 = pl.ds(SEQ + r4, BLK), pl.ds(2 * SEQ + r4, BLK)
            x0, x1, x2 = mx_s[nat, :], mx_s[p1, :], mx_s[p2, :]
            m_all = jnp.maximum(jnp.maximum(x0, x1), x2)
            w0, w1, w2 = jnp.exp2(x0 - m_all), jnp.exp2(x1 - m_all), jnp.exp2(x2 - m_all)
            num = w0 * num_s[nat, :] + w1 * num_s[p1, :] + w2 * num_s[p2, :]
            den = w0 * den_s[nat, :] + w1 * den_s[p1, :] + w2 * den_s[p2, :]
            xq[nat, :] = num / den
    o_ref[...] = xq[...].astype(BF16)


def _attention(zq, slopes):
    t = zq.shape[2]
    nb = t // SEQ
    col = 2 * HEAD_DIM
    qkv_spec = lambda kind: pl.BlockSpec((None, HP_PER_STEP, SEQ, col),
                                         lambda b, h, kind=kind: (kind, h, b, 0))
    f32_seq = lambda: pltpu.VMEM((SEQ, col), F32)
    f32_all = lambda: pltpu.VMEM((N_PAT * SEQ, col), F32)
    return pl.pallas_call(
        _attn_kernel,
        grid=(nb, HEADS // 2 // HP_PER_STEP),
        in_specs=[
            pl.BlockSpec((HP_PER_STEP, 2, col), lambda b, h: (h, 0, 0)),
            qkv_spec(0), qkv_spec(1), qkv_spec(2),
        ],
        out_specs=pl.BlockSpec((HP_PER_STEP, SEQ, col), lambda b, h: (h, b, 0)),
        out_shape=jax.ShapeDtypeStruct((HEADS // 2, t, col), BF16),
        scratch_shapes=[
            f32_seq(), f32_seq(), f32_seq(),
            f32_seq(), f32_seq(), f32_seq(),
            pltpu.VMEM(((N_PAT - 1) * SEQ, col), BF16),
            pltpu.VMEM(((N_PAT - 1) * SEQ, col), BF16),
            pltpu.VMEM(((N_PAT - 1) * SEQ, col), BF16),
            pltpu.VMEM((2, 2 * BLK, col), BF16),
            pltpu.VMEM((N_PAT - 1, BLK, 4 * BLK), F32),
            pltpu.VMEM((N_PAT, BLK, 2 * BLK), F32),
            f32_all(), f32_all(), f32_all(),
        ] * HP_PER_STEP,
        compiler_params=pltpu.CompilerParams(
            dimension_semantics=("arbitrary", "arbitrary"),
            vmem_limit_bytes=VMEM_LIMIT_BYTES),
        name="dilated_attn",
    )(slopes, zq, zq, zq)


def _merge_kernel(u_ref, v_ref, ga_ref, gb_ref, yb_ref, x_ref, ws_ref, bs_ref,
                  wa_ref, wb_ref, wo_ref, g_ref, o_ref, ya_ref):
    tm = u_ref.shape[0]
    ti = lax.broadcasted_iota(jnp.int32, (CHUNK, CHUNK), 0)
    si = lax.broadcasted_iota(jnp.int32, (CHUNK, CHUNK), 1)
    causal = si <= ti
    for g in range(GROUPS):
        w = jnp.where(causal, ws_ref[g], 0.0).astype(BF16)
        bcol = bs_ref[:, g:g + 1]
        cols = slice(g * GROUP_DIM, (g + 1) * GROUP_DIM)
        for c in range(tm // CHUNK):
            rows = slice(c * CHUNK, (c + 1) * CHUNK)
            mixed = jnp.dot(w, v_ref[rows, cols], preferred_element_type=F32) + bcol
            ya_ref[rows, cols] = (u_ref[rows, cols].astype(F32) * mixed).astype(BF16)
    a = jnp.dot(ya_ref[...], wa_ref[...].astype(BF16), preferred_element_type=F32)
    yb = jnp.concatenate([yb_ref[hp] for hp in range(HEADS // 2)], axis=1)
    b = jnp.dot(yb, wb_ref[...].astype(BF16), preferred_element_type=F32)
    merged = ga_ref[...].astype(F32) * a + gb_ref[...].astype(F32) * b
    y = jnp.dot(merged.astype(BF16), wo_ref[...].astype(BF16), preferred_element_type=F32)
    o_ref[...] = x_ref[...] + _rms_norm(y, g_ref[...])


def _merge(z, yb, x2, w_s, b_s_t, w_a, w_b, w_o, g, tm):
    t = x2.shape[0]
    row = lambda c: pl.BlockSpec((tm, D_MODEL), lambda i, c=c: (i, c))
    const = lambda shape, **kw: pl.BlockSpec(shape, lambda i: (0,) * len(shape), **kw)
    weight = lambda: const((D_MODEL, D_MODEL), pipeline_mode=pl.Buffered(1))
    return pl.pallas_call(
        _merge_kernel,
        grid=(t // tm,),
        in_specs=[row(0), row(1), row(2), row(3),
                  pl.BlockSpec((HEADS // 2, tm, 2 * HEAD_DIM), lambda i: (0, i, 0)), row(0),
                  const((GROUPS, CHUNK, CHUNK)), const((CHUNK, GROUPS)),
                  weight(), weight(), weight(), const((1, D_MODEL))],
        out_specs=row(0),
        out_shape=jax.ShapeDtypeStruct((t, D_MODEL), F32),
        scratch_shapes=[pltpu.VMEM((tm, D_MODEL), BF16)],
        compiler_params=pltpu.CompilerParams(
            dimension_semantics=("arbitrary",),
            vmem_limit_bytes=VMEM_LIMIT_BYTES),
        name="merge",
    )(z, z, z, z, yb, x2, w_s, b_s_t, w_a, w_b, w_o, g)


def _ffn_kernel(x_ref, gpre_ref, w1_ref, w2_ref, gpost_ref, o_ref):
    x = x_ref[...]
    h = _rms_norm(x, gpre_ref[...]).astype(BF16)
    y = None
    for j in range(D_FF // D_MODEL):
        slab = slice(j * D_MODEL, (j + 1) * D_MODEL)
        f = jnp.dot(h, w1_ref[:, slab].astype(BF16), preferred_element_type=F32)
        f = jnp.square(jnp.maximum(f, 0.0)).astype(BF16)
        part = jnp.dot(f, w2_ref[slab, :].astype(BF16), preferred_element_type=F32)
        y = part if y is None else y + part
    o_ref[...] = x + _rms_norm(y, gpost_ref[...])


def _ffn(x1, g_pre, w1, w2, g_post, tm):
    t = x1.shape[0]
    const = lambda shape, **kw: pl.BlockSpec(shape, lambda i: (0,) * len(shape), **kw)
    return pl.pallas_call(
        _ffn_kernel,
        grid=(t // tm,),
        in_specs=[
            pl.BlockSpec((tm, D_MODEL), lambda i: (i, 0)),
            const((1, D_MODEL)),
            const((D_MODEL, D_FF), pipeline_mode=pl.Buffered(1)),
            const((D_FF, D_MODEL), pipeline_mode=pl.Buffered(1)),
            const((1, D_MODEL)),
        ],
        out_specs=pl.BlockSpec((tm, D_MODEL), lambda i: (i, 0)),
        out_shape=jax.ShapeDtypeStruct((t, D_MODEL), F32),
        compiler_params=pltpu.CompilerParams(
            dimension_semantics=("arbitrary",),
            vmem_limit_bytes=VMEM_LIMIT_BYTES),
        name="ffn",
    )(x1, g_pre, w1, w2, g_post)


def kernel(x, norm_mix_pre, w_in, b_gate, ln_v_g, ln_v_b, w_s, b_s, w_a_proj, w_b_proj,
           w_out, norm_mix_post, norm_ffn_pre, w_ff1, w_ff2, norm_ffn_post):
    bsz, s, d = x.shape
    assert (s, d) == (SEQ, D_MODEL)
    depth = w_in.shape[0]
    slopes = jnp.exp2(-8.0 * jnp.arange(1, HEADS + 1, dtype=F32) / HEADS)
    slopes = jnp.broadcast_to(slopes.reshape(HEADS // 2, 2, 1), (HEADS // 2, 2, 2 * HEAD_DIM))
    x2 = x.reshape(bsz * s, d)
    for l in range(depth):
        z, zq = _inproj(x2, norm_mix_pre[l][None], w_in[l], ln_v_g[l][None],
                        ln_v_b[l][None], b_gate[l], tm=512)
        yb = _attention(zq, slopes)
        x2 = _merge(z, yb, x2, w_s[l], b_s[l].T, w_a_proj[l], w_b_proj[l], w_out[l],
                    norm_mix_post[l][None], tm=512)
        x2 = _ffn(x2, norm_ffn_pre[l][None], w_ff1[l], w_ff2[l], norm_ffn_post[l][None],
                  tm=512)
    return x2.reshape(bsz, s, d)
```
